```python
import math
import jax
import jax.numpy as jnp
from jax import lax
import numpy as np

D_MODEL = 2048
BATCH = 2
SEQ = 16384
DEPTH = 1
DEC_BATCH = 8
DEC_SEQ = 64
PAST_LEN = 1024

CHUNK = 64
MIX_WIDTH = D_MODEL
N_HEADS_DIFF = 8
HEAD_DIM_DIFF_V = MIX_WIDTH // 2 // N_HEADS_DIFF
HEAD_DIM_DIFF_QK = HEAD_DIM_DIFF_V // 2
ROT_DIM = HEAD_DIM_DIFF_QK // 4
ROPE_THETA = 500000.0
DIFF_W = N_HEADS_DIFF * HEAD_DIM_DIFF_V
N_HEADS_GDN = 8
HEAD_DIM_GDN = MIX_WIDTH // 2 // N_HEADS_GDN
GDN_W = N_HEADS_GDN * HEAD_DIM_GDN
CONV_K = 4
IN_SPLITS = (DIFF_W, 2 * DIFF_W, 3 * DIFF_W, 3 * DIFF_W + 3 * GDN_W,
             3 * DIFF_W + 3 * GDN_W + N_HEADS_GDN, 3 * DIFF_W + 3 * GDN_W + 2 * N_HEADS_GDN)
IN_COLS = 3 * DIFF_W + 3 * GDN_W + 2 * N_HEADS_GDN + GDN_W
N_EXPERTS = 32
TOP_K = 4
D_FF = D_MODEL
SWIGLU_LIMIT = 7.0
SWIGLU_ALPHA = 1.702
EXPERT_BLOCK = 128
Q_BLOCK = 128
EPS = 1e-6
SUBLN_EPS = 1e-5

kernel_name = 'diff_gdn_hybrid_moe_stream_step'


def rms_norm(x, w, eps):
    xf = x.astype(jnp.float32)
    y = xf * lax.rsqrt(jnp.mean(xf * xf, axis=-1, keepdims=True) + eps)
    return (y * w.astype(jnp.float32)).astype(x.dtype)


def l2_normalize(x):
    return x * lax.rsqrt(jnp.sum(x * x, axis=-1, keepdims=True) + 1e-6)


def partial_rope(x, pos):
    half = ROT_DIM // 2
    inv_freq = jnp.power(ROPE_THETA, -jnp.arange(half, dtype=jnp.float32) * (2.0 / ROT_DIM))
    ang = pos.astype(jnp.float32)[:, None] * inv_freq[None, :]
    cos = jnp.cos(ang)[None, :, None, None, :]
    sin = jnp.sin(ang)[None, :, None, None, :]
    xr = x[..., :ROT_DIM].astype(jnp.float32)
    x1, x2 = xr[..., :half], xr[..., half:]
    rot = jnp.concatenate([x1 * cos - x2 * sin, x2 * cos + x1 * sin], axis=-1).astype(x.dtype)
    return jnp.concatenate([rot, x[..., ROT_DIM:]], axis=-1)


def diff_lambda(lq1, lk1, lq2, lk2, layer):
    lam_init = 0.8 - 0.6 * math.exp(-0.3 * layer)
    lam = (jnp.exp(jnp.sum(lq1.astype(jnp.float32) * lk1.astype(jnp.float32)))
           - jnp.exp(jnp.sum(lq2.astype(jnp.float32) * lk2.astype(jnp.float32))) + lam_init)
    return lam, lam_init


def diff_weighted_values(scores, v, lam):
    p = jax.nn.softmax(scores, axis=-1)
    p = p[:, :, 0] - lam * p[:, :, 1]
    return jnp.einsum('bhqk,bkhd->bqhd', p, v.astype(jnp.float32))


def diff_attn_prompt(q, k, v, lam):
    b, t, h, _, _ = q.shape
    n_blocks = t // Q_BLOCK
    key_chunk = jnp.arange(t) // CHUNK
    scale = HEAD_DIM_DIFF_QK ** -0.5

    def one_block(i):
        start = i * Q_BLOCK
        qb = lax.dynamic_slice_in_dim(q, start, Q_BLOCK, axis=1)
        s = jnp.einsum('bqhmd,bkhmd->bhmqk', qb, k).astype(jnp.float32) * scale
        q_chunk = (start + jnp.arange(Q_BLOCK)) // CHUNK
        visible = key_chunk[None, :] <= q_chunk[:, None]
        s = jnp.where(visible, s, -jnp.inf)
        return diff_weighted_values(s, v, lam)

    out = lax.map(one_block, jnp.arange(n_blocks))
    return out.transpose(1, 0, 2, 3, 4).reshape(b, t, h, HEAD_DIM_DIFF_V)


def diff_attn_sample(q, k_new, v_new, k_cache, v_cache, lam):
    b, t, h, _, _ = q.shape
    k_all = jnp.concatenate([k_cache.reshape(b, -1, h, 2, HEAD_DIM_DIFF_QK).astype(k_new.dtype), k_new], axis=1)
    v_all = jnp.concatenate([v_cache.astype(v_new.dtype), v_new], axis=1)
    s = jnp.einsum('bqhmd,bkhmd->bhmqk', q, k_all).astype(jnp.float32) * (HEAD_DIM_DIFF_QK ** -0.5)
    return diff_weighted_values(s, v_all, lam)


def causal_conv_silu(x, past, w):
    t = x.shape[1]
    xp = jnp.concatenate([past.astype(x.dtype), x], axis=1)
    y = xp[:, 0:t] * w[:, 0]
    for j in range(1, CONV_K):
        y = y + xp[:, j:j + t] * w[:, j]
    return jax.nn.silu(y), xp[:, t:]


def gated_delta_chunked(q, k, v, g, beta, s0):
    b, t, h, dk = q.shape
    dv = v.shape[-1]
    n = -(-t // CHUNK)
    pad = n * CHUNK - t

    def to_chunks(x):
        x = jnp.pad(x, [(0, 0), (0, pad)] + [(0, 0)] * (x.ndim - 2))
        x = jnp.moveaxis(x, 2, 1)
        return x.reshape((b, h, n, CHUNK) + x.shape[3:])

    q, k, v, g, beta = (to_chunks(a) for a in (q, k, v, g, beta))
    gc = jnp.cumsum(g, axis=-1)
    idx = jnp.arange(CHUNK)
    incl = idx[:, None] >= idx[None, :]
    strict = idx[:, None] > idx[None, :]
    diff = gc[..., :, None] - gc[..., None, :]
    decay = jnp.where(incl, jnp.exp(jnp.where(incl, diff, 0.0)), 0.0)
    k_beta = k * beta[..., None]
    a = jnp.where(strict, jnp.einsum('bhncd,bhnkd->bhnck', k_beta, k) * decay, 0.0)
    lower = a + jnp.eye(CHUNK, dtype=a.dtype)
    rhs = jnp.concatenate([v * beta[..., None], k_beta * jnp.exp(gc)[..., None]], axis=-1)
    sol = lax.linalg.triangular_solve(lower, rhs, left_side=True, lower=True, unit_diagonal=True)
    u, w = sol[..., :dv], sol[..., dv:]
    qk = jnp.where(incl, jnp.einsum('bhncd,bhnkd->bhnck', q, k) * decay, 0.0)

    def step(s, blk):
        q_c, k_c, u_c, w_c, gc_c, qk_c = blk
        v_new = u_c - jnp.einsum('bhck,bhkv->bhcv', w_c, s)
        o = (jnp.einsum('bhck,bhkv->bhcv', q_c * jnp.exp(gc_c)[..., None], s)
             + jnp.einsum('bhcj,bhjv->bhcv', qk_c, v_new))
        g_last = gc_c[..., -1]
        k_dec = k_c * jnp.exp(g_last[..., None] - gc_c)[..., None]
        s = s * jnp.exp(g_last)[..., None, None] + jnp.einsum('bhck,bhcv->bhkv', k_dec, v_new)
        return s, o

    blocks = tuple(jnp.moveaxis(a, 2, 0) for a in (q, k, u, w, gc, qk))
    s_final, o = lax.scan(step, s0, blocks)
    o = o.transpose(1, 0, 3, 2, 4).reshape(b, n * CHUNK, h, dv)[:, :t]
    return o, s_final


def mixer_layer(x, pos, k_cache, v_cache, conv_past, s0, attn_norm_w, w_in, conv_w, a_log, dt_bias,
                lam, lam_init, diff_norm_w, gdn_norm_w, w_out):
    b, t, _ = x.shape
    h = rms_norm(x, attn_norm_w, EPS)
    proj = h @ w_in
    q_d, k_d, v_d, qkv_g, a_g, b_g, z_g = jnp.split(proj, IN_SPLITS, axis=-1)
    q_d = partial_rope(q_d.reshape(b, t, N_HEADS_DIFF, 2, HEAD_DIM_DIFF_QK), pos)
    k_d = partial_rope(k_d.reshape(b, t, N_HEADS_DIFF, 2, HEAD_DIM_DIFF_QK), pos)
    v_d = v_d.reshape(b, t, N_HEADS_DIFF, HEAD_DIM_DIFF_V)
    if k_cache is None:
        o_d = diff_attn_prompt(q_d, k_d, v_d, lam)
    else:
        o_d = diff_attn_sample(q_d, k_d, v_d, k_cache, v_cache, lam)
    o_d = rms_norm(o_d.astype(x.dtype), diff_norm_w, SUBLN_EPS) * (1.0 - lam_init)
    qkv_c, conv_new = causal_conv_silu(qkv_g, conv_past, conv_w)
    q_g, k_g, v_g = jnp.split(qkv_c.astype(jnp.float32), 3, axis=-1)
    q_g = l2_normalize(q_g.reshape(b, t, N_HEADS_GDN, HEAD_DIM_GDN)) * (HEAD_DIM_GDN ** -0.5)
    k_g = l2_normalize(k_g.reshape(b, t, N_HEADS_GDN, HEAD_DIM_GDN))
    v_g = v_g.reshape(b, t, N_HEADS_GDN, HEAD_DIM_GDN)
    beta = jax.nn.sigmoid(b_g.astype(jnp.float32))
    g = -jnp.exp(a_log.astype(jnp.float32)) * jax.nn.softplus(a_g.astype(jnp.float32) + dt_bias.astype(jnp.float32))
    o_g, s_new = gated_delta_chunked(q_g, k_g, v_g, g, beta, s0.astype(jnp.float32))
    o_g = rms_norm(o_g.astype(x.dtype), gdn_norm_w, EPS) * jax.nn.silu(z_g.reshape(b, t, N_HEADS_GDN, HEAD_DIM_GDN))
    mix = jnp.concatenate([o_d.reshape(b, t, DIFF_W), o_g.reshape(b, t, GDN_W)], axis=-1)
    x = x + mix @ w_out
    k_rows = k_d.reshape(b, t, N_HEADS_DIFF, 2 * HEAD_DIM_DIFF_QK)
    return x, k_rows, v_d, conv_new, s_new.astype(x.dtype)


def moe_ffn(h, router_w, router_b, w_gate, b_gate, w_up, b_up, w_down, b_down):
    n_tok, d = h.shape
    logits = (h @ router_w + router_b).astype(jnp.float32)
    top_val, top_idx = lax.top_k(logits, TOP_K)
    gates = jax.nn.softmax(top_val, axis=-1)
    n_assign = n_tok * TOP_K
    flat_e = top_idx.reshape(n_assign)
    flat_tok = jnp.arange(n_assign, dtype=jnp.int32) // TOP_K
    flat_gate = gates.reshape(n_assign)
    order = jnp.argsort(flat_e)
    e_sorted = flat_e[order]
    counts = jnp.bincount(flat_e, length=N_EXPERTS)
    padded = (counts + EXPERT_BLOCK - 1) // EXPERT_BLOCK * EXPERT_BLOCK
    start = jnp.cumsum(counts) - counts
    padded_end = jnp.cumsum(padded)
    padded_start = padded_end - padded
    dest = padded_start[e_sorted] + (jnp.arange(n_assign) - start[e_sorted])
    n_blocks = -(-n_assign // EXPERT_BLOCK) + N_EXPERTS
    slot_tok = jnp.full((n_blocks * EXPERT_BLOCK,), n_tok, jnp.int32).at[dest].set(flat_tok[order])
    slot_gate = jnp.zeros((n_blocks * EXPERT_BLOCK,), jnp.float32).at[dest].set(flat_gate[order])
    block_e = jnp.minimum(jnp.searchsorted(padded_end, jnp.arange(n_blocks) * EXPERT_BLOCK, side='right'),
                          N_EXPERTS - 1)
    h_pad = jnp.concatenate([h, jnp.zeros((1, d), h.dtype)], axis=0)

    def expert_block(args):
        tok_b, e = args
        xb = h_pad[tok_b]
        gate = jnp.minimum(xb @ w_gate[e] + b_gate[e], SWIGLU_LIMIT)
        up = jnp.clip(xb @ w_up[e] + b_up[e], -SWIGLU_LIMIT, SWIGLU_LIMIT)
        act = (up + 1.0) * gate * jax.nn.sigmoid(SWIGLU_ALPHA * gate)
        return act @ w_down[e] + b_down[e]

    y_slots = lax.map(expert_block, (slot_tok.reshape(n_blocks, EXPERT_BLOCK), block_e))
    y_slots = y_slots.reshape(-1, d) * slot_gate[:, None].astype(h.dtype)
    return jnp.zeros((n_tok + 1, d), h.dtype).at[slot_tok].add(y_slots)[:n_tok]


def moe_layer(x, ffn_norm_w, router_w, router_b, w_gate, b_gate, w_up, b_up, w_down, b_down):
    b, t, d = x.shape
    h = rms_norm(x, ffn_norm_w, EPS).reshape(b * t, d)
    return x + moe_ffn(h, router_w, router_b, w_gate, b_gate, w_up, b_up, w_down, b_down).reshape(b, t, d)


def setup_inputs(seed: int = 0) -> dict:
    key = jax.random.key(seed)
    ks = jax.random.split(key, 32)
    f32 = jnp.float32

    def nrm(k, shape, scale):
        return jax.random.normal(k, shape, f32) * scale

    dt = jnp.exp(jax.random.uniform(ks[10], (DEPTH, N_HEADS_GDN), f32, math.log(1e-3), math.log(1e-1)))
    return {
        'x_prompt': nrm(ks[0], (BATCH, SEQ, D_MODEL), 1.0),
        'x_sample': nrm(ks[1], (DEC_BATCH, DEC_SEQ, D_MODEL), 1.0),
        'cache_k_diff': nrm(ks[2], (DEPTH, DEC_BATCH, PAST_LEN, N_HEADS_DIFF, 2 * HEAD_DIM_DIFF_QK), 1.0),
        'cache_v_diff': nrm(ks[3], (DEPTH, DEC_BATCH, PAST_LEN, N_HEADS_DIFF, HEAD_DIM_DIFF_V), 1.0),
        'state_conv': nrm(ks[4], (DEPTH, DEC_BATCH, CONV_K - 1, 3 * GDN_W), 1.0),
        'state_gdn': nrm(ks[5], (DEPTH, DEC_BATCH, N_HEADS_GDN, HEAD_DIM_GDN, HEAD_DIM_GDN), 0.1),
        'attn_norm_w': 1.0 + nrm(ks[6], (DEPTH, D_MODEL), 0.02),
        'w_in': nrm(ks[7], (DEPTH, D_MODEL, IN_COLS), D_MODEL ** -0.5),
        'conv_w': nrm(ks[8], (DEPTH, 3 * GDN_W, CONV_K), CONV_K ** -0.5),
        'a_log': jnp.log(jax.random.uniform(ks[9], (DEPTH, N_HEADS_GDN), f32, 1.0, 16.0)),
        'dt_bias': jnp.log(jnp.expm1(dt)),
        'lambda_q1': nrm(ks[11], (DEPTH, HEAD_DIM_DIFF_QK), 0.1),
        'lambda_k1': nrm(ks[12], (DEPTH, HEAD_DIM_DIFF_QK), 0.1),
        'lambda_q2': nrm(ks[13], (DEPTH, HEAD_DIM_DIFF_QK), 0.1),
        'lambda_k2': nrm(ks[14], (DEPTH, HEAD_DIM_DIFF_QK), 0.1),
        'diff_norm_w': 1.0 + nrm(ks[15], (DEPTH, HEAD_DIM_DIFF_V), 0.02),
        'gdn_norm_w': 1.0 + nrm(ks[16], (DEPTH, HEAD_DIM_GDN), 0.02),
        'w_out': nrm(ks[17], (DEPTH, MIX_WIDTH, D_MODEL), MIX_WIDTH ** -0.5),
        'ffn_norm_w': 1.0 + nrm(ks[18], (DEPTH, D_MODEL), 0.02),
        'router_w': nrm(ks[19], (DEPTH, D_MODEL, N_EXPERTS), D_MODEL ** -0.5),
        'router_b': nrm(ks[20], (DEPTH, N_EXPERTS), 0.01),
        'w_gate': nrm(ks[21], (DEPTH, N_EXPERTS, D_MODEL, D_FF), D_MODEL ** -0.5),
        'b_gate': nrm(ks[22], (DEPTH, N_EXPERTS, D_FF), 0.01),
        'w_up': nrm(ks[23], (DEPTH, N_EXPERTS, D_MODEL, D_FF), D_MODEL ** -0.5),
        'b_up': nrm(ks[24], (DEPTH, N_EXPERTS, D_FF), 0.01),
        'w_down': nrm(ks[25], (DEPTH, N_EXPERTS, D_FF, D_MODEL), D_FF ** -0.5),
        'b_down': nrm(ks[26], (DEPTH, N_EXPERTS, D_MODEL), 0.01),
        'final_norm_w': 1.0 + nrm(ks[27], (D_MODEL,), 0.02),
    }


def reference(x_prompt, x_sample, cache_k_diff, cache_v_diff, state_conv, state_gdn,
              attn_norm_w, w_in, conv_w, a_log, dt_bias, lambda_q1, lambda_k1, lambda_q2, lambda_k2,
              diff_norm_w, gdn_norm_w, w_out, ffn_norm_w, router_w, router_b,
              w_gate, b_gate, w_up, b_up, w_down, b_down, final_norm_w):
    past_len = cache_k_diff.shape[2]
    pos_p = jnp.arange(x_prompt.shape[1])
    pos_s = past_len + jnp.arange(x_sample.shape[1])
    x_p, x_s = x_prompt, x_sample
    kp_l, vp_l, cp_l, sp_l, ks_l, vs_l, cs_l, ss_l = [], [], [], [], [], [], [], []
    for l in range(DEPTH):
        lam, lam_init = diff_lambda(lambda_q1[l], lambda_k1[l], lambda_q2[l], lambda_k2[l], l)
        mix_w = (attn_norm_w[l], w_in[l], conv_w[l], a_log[l], dt_bias[l], lam, lam_init,
                 diff_norm_w[l], gdn_norm_w[l], w_out[l])
        moe_w = (ffn_norm_w[l], router_w[l], router_b[l], w_gate[l], b_gate[l], w_up[l], b_up[l],
                 w_down[l], b_down[l])
        conv0 = jnp.zeros((x_p.shape[0], CONV_K - 1, 3 * GDN_W), x_p.dtype)
        s0 = jnp.zeros((x_p.shape[0], N_HEADS_GDN, HEAD_DIM_GDN, HEAD_DIM_GDN), jnp.float32)
        x_p, kp, vp, cp, sp = mixer_layer(x_p, pos_p, None, None, conv0, s0, *mix_w)
        x_s, ks_, vs_, cs_, ss_ = mixer_layer(x_s, pos_s, cache_k_diff[l], cache_v_diff[l],
                                              state_conv[l], state_gdn[l], *mix_w)
        x_p = moe_layer(x_p, *moe_w)
        x_s = moe_layer(x_s, *moe_w)
        kp_l.append(kp); vp_l.append(vp); cp_l.append(cp); sp_l.append(sp)
        ks_l.append(ks_); vs_l.append(vs_); cs_l.append(cs_); ss_l.append(ss_)
    y_prompt = rms_norm(x_p, final_norm_w, EPS)
    y_sample = rms_norm(x_s, final_norm_w, EPS)
    k_diff_prompt = jnp.stack(kp_l)
    v_diff_prompt = jnp.stack(vp_l)
    conv_prompt = jnp.stack(cp_l)
    gdn_prompt = jnp.stack(sp_l)
    k_diff_sample = jnp.stack(ks_l)
    v_diff_sample = jnp.stack(vs_l)
    conv_sample = jnp.stack(cs_l)
    gdn_sample = jnp.stack(ss_l)
    return (y_prompt, y_sample, k_diff_prompt, v_diff_prompt, conv_prompt, gdn_prompt,
            k_diff_sample, v_diff_sample, conv_sample, gdn_sample)
```

```python
import functools
import math

import jax
import jax.numpy as jnp
from jax import lax
from jax.experimental import pallas as pl
from jax.experimental.pallas import tpu as pltpu

F32 = jnp.float32
BF16 = jnp.bfloat16

D_MODEL = 2048
N_HEADS = 8
HEAD = 128
QK_DIM = 64
ROT_DIM = 16
ROT_HALF = ROT_DIM // 2
ROPE_THETA = 500000.0
CHUNK = 64
GROUP_W = N_HEADS * HEAD
CONV_K = 4
N_EXPERTS = 32
TOP_K = 4
D_FF = 2048
SWIGLU_LIMIT = 7.0
SWIGLU_ALPHA = 1.702
EPS = 1e-6
SUBLN_EPS = 1e-5
LAM_INIT = 0.8 - 0.6 * math.exp(-0.3 * 0)

LANES = 128
SUBLANES = 8
VMEM_LIMIT = 56 * 1024 * 1024

PROJ_TN = 512
PROJ_COLS = 7680
COL_QKVG = 3 * GROUP_W
COL_Z = 6 * GROUP_W
COL_AB = 7 * GROUP_W

EXPERT_BLOCK = 512
FFN_TF = 1024


def _cparams(sem, vmem=VMEM_LIMIT):
    return pltpu.CompilerParams(dimension_semantics=sem, vmem_limit_bytes=vmem)


def _inproj_kernel(x_ref, nw_ref, w_ref, c_ref, s1_ref, s2_ref, proj_ref, qkv_ref, h_ref):
    j = pl.program_id(1)

    @pl.when(j == 0)
    def _():
        x = x_ref[...]
        ms = jnp.mean(x * x, axis=-1, keepdims=True)
        h_ref[...] = (x * lax.rsqrt(ms + EPS) * nw_ref[...]).astype(BF16)

    acc = jnp.dot(h_ref[...], w_ref[...], preferred_element_type=F32)

    @pl.when(j < 4)
    def _():
        c = c_ref[...]
        s1 = s1_ref[...]
        s2 = s2_ref[...]
        scale = jnp.where(j < 2, QK_DIM ** -0.5, 1.0).astype(F32)
        for g in range(PROJ_TN // LANES):
            xg = acc[:, g * LANES:(g + 1) * LANES]
            r = (xg * c + pltpu.roll(xg, LANES - ROT_HALF, 1) * s1
                 + pltpu.roll(xg, ROT_HALF, 1) * s2)
            proj_ref[:, g * LANES:(g + 1) * LANES] = r
            qkv_ref[:, g * LANES:(g + 1) * LANES] = (r * scale).astype(BF16)

    @pl.when(jnp.logical_and(j >= 4, j < 6))
    def _():
        proj_ref[...] = acc
        qkv_ref[...] = acc.astype(BF16)

    @pl.when(j >= 6)
    def _():
        proj_ref[...] = acc


def _rope_tables(pos):
    inv_freq = jnp.power(ROPE_THETA, -jnp.arange(ROT_HALF, dtype=F32) * (2.0 / ROT_DIM))
    ang = pos.astype(F32)[:, None] * inv_freq[None, :]
    cos, sin = jnp.cos(ang), jnp.sin(ang)
    t = pos.shape[0]
    one = jnp.ones((t, QK_DIM - ROT_DIM), F32)
    zero = jnp.zeros((t, QK_DIM - ROT_DIM), F32)
    z8 = jnp.zeros((t, ROT_HALF), F32)
    c = jnp.concatenate([cos, cos, one], axis=1)
    s1 = jnp.concatenate([-sin, z8, zero], axis=1)
    s2 = jnp.concatenate([z8, sin, zero], axis=1)
    return tuple(jnp.concatenate([a, a], axis=1) for a in (c, s1, s2))


def _inproj(x2d, norm_w, w_perm, tables):
    m = x2d.shape[0]
    tm = min(1024, m)
    tab_rows = tables[0].shape[0]
    n_tab = tab_rows // tm
    grid = (m // tm, PROJ_COLS // PROJ_TN)
    tab_spec = pl.BlockSpec((tm, LANES), lambda i, j: (i % n_tab, 0))
    return pl.pallas_call(
        _inproj_kernel,
        out_shape=(jax.ShapeDtypeStruct((m, PROJ_COLS), F32),
                   jax.ShapeDtypeStruct((m, 3 * GROUP_W), BF16)),
        grid=grid,
        in_specs=[pl.BlockSpec((tm, D_MODEL), lambda i, j: (i, 0)),
                  pl.BlockSpec((1, D_MODEL), lambda i, j: (0, 0)),
                  pl.BlockSpec((D_MODEL, PROJ_TN), lambda i, j: (0, j)),
                  tab_spec, tab_spec, tab_spec],
        out_specs=(pl.BlockSpec((tm, PROJ_TN), lambda i, j: (i, j)),
                   pl.BlockSpec((tm, PROJ_TN), lambda i, j: (i, jnp.minimum(j, 5)))),
        scratch_shapes=[pltpu.VMEM((tm, D_MODEL), BF16)],
        compiler_params=_cparams(("arbitrary", "arbitrary")),
        name="inproj",
    )(x2d, norm_w.reshape(1, D_MODEL), w_perm, *tables)


def _attn_kernel(lam_ref, q_ref, k_ref, vt_ref, nw_ref, o_ref, qq_ref, m_ref, l_ref, acc_ref,
                 *, tq, tk, causal):
    i = pl.program_id(2)
    q = q_ref[0]
    lane = lax.broadcasted_iota(jnp.int32, (tq, LANES), 1)
    zero = jnp.zeros_like(q)
    qq_ref[0:tq, :] = jnp.where(lane < QK_DIM, q, zero)
    qq_ref[tq:2 * tq, :] = jnp.where(lane >= QK_DIM, q, zero)
    m_ref[...] = jnp.full(m_ref.shape, -jnp.inf, F32)
    l_ref[...] = jnp.zeros(l_ref.shape, F32)
    acc_ref[...] = jnp.zeros(acc_ref.shape, F32)

    def step(j, masked):
        kb = k_ref[0, pl.ds(pl.multiple_of(j * tk, tk), tk), :]
        s = lax.dot_general(kb, qq_ref[...], (((1,), (1,)), ((), ())),
                            preferred_element_type=F32)
        if masked:
            kchunk = (j * tk + lax.broadcasted_iota(jnp.int32, (tk, 2 * tq), 0)) // CHUNK
            qcol = lax.broadcasted_iota(jnp.int32, (tk, 2 * tq), 1) & (tq - 1)
            qchunk = (i * tq + qcol) // CHUNK
            s = jnp.where(kchunk <= qchunk, s, -jnp.inf)
        m_old = m_ref[...]
        m_new = jnp.maximum(m_old, jnp.max(s, axis=0, keepdims=True))
        alpha = jnp.exp(m_old - m_new)
        p = jnp.exp(s - m_new)
        l_ref[...] = alpha * l_ref[...] + jnp.sum(p, axis=0, keepdims=True)
        m_ref[...] = m_new
        pv = jnp.dot(vt_ref[0, 0, j], p.astype(BF16), preferred_element_type=F32)
        acc_ref[...] = acc_ref[...] * alpha + pv

    if causal:
        r = tq // tk
        n_full = i * r

        def body(j, carry):
            step(j, False)
            return carry

        lax.fori_loop(0, n_full, body, 0)
        for d in range(r):
            step(n_full + d, True)
    else:
        step(0, False)

    lam = lam_ref[0, 0]
    inv_l = 1.0 / l_ref[...]
    o = acc_ref[...] * inv_l
    o = o[:, 0:tq] - lam * o[:, tq:2 * tq]
    ms = jnp.mean(o * o, axis=0, keepdims=True)
    on = (o * lax.rsqrt(ms + SUBLN_EPS)).T
    o_ref[0] = ((on * nw_ref[...]) * (1.0 - LAM_INIT)).astype(BF16)


def _diff_attn(lam, q_src, q_col0, k_src, k_col0, vt, norm_w, *, tq, tk, causal):
    b, t_q = q_src.shape[0], q_src.shape[1]
    t_k = k_src.shape[1]
    nkb = vt.shape[2]
    kern = functools.partial(_attn_kernel, tq=tq, tk=tk, causal=causal)
    return pl.pallas_call(
        kern,
        out_shape=jax.ShapeDtypeStruct((b, t_q, GROUP_W), BF16),
        grid=(b, N_HEADS, t_q // tq),
        in_specs=[pl.BlockSpec(memory_space=pltpu.SMEM),
                  pl.BlockSpec((1, tq, HEAD), lambda bi, h, i: (bi, i, q_col0 + h)),
                  pl.BlockSpec((1, t_k, HEAD), lambda bi, h, i: (bi, 0, k_col0 + h)),
                  pl.BlockSpec((1, 1, nkb, HEAD, tk), lambda bi, h, i: (bi, h, 0, 0, 0)),
                  pl.BlockSpec((1, HEAD), lambda bi, h, i: (0, 0))],
        out_specs=pl.BlockSpec((1, tq, HEAD), lambda bi, h, i: (bi, i, h)),
        scratch_shapes=[pltpu.VMEM((2 * tq, HEAD), BF16),
                        pltpu.VMEM((1, 2 * tq), F32),
                        pltpu.VMEM((1, 2 * tq), F32),
                        pltpu.VMEM((HEAD, 2 * tq), F32)],
        compiler_params=_cparams(("arbitrary", "arbitrary", "arbitrary")),
        name="diff_attn",
    )(lam, q_src, k_src, vt, norm_w.reshape(1, HEAD))


def _bdot(a, b):
    return jnp.dot(a.astype(BF16), b.astype(BF16), preferred_element_type=F32)


def _bdot_nt(a, b):
    return lax.dot_general(a.astype(BF16), b.astype(BF16), (((1,), (1,)), ((), ())),
                           preferred_element_type=F32)


def _bdot_tn(a, b):
    return lax.dot_general(a.astype(BF16), b.astype(BF16), (((0,), (0,)), ((), ())),
                           preferred_element_type=F32)


def _split3(a):
    hi = a.astype(BF16)
    r = a - hi.astype(F32)
    mid = r.astype(BF16)
    lo = (r - mid.astype(F32)).astype(BF16)
    return hi, mid, lo


def _dot_hp(a, b):
    a_hi = a.astype(BF16)
    a_lo = (a - a_hi.astype(F32)).astype(BF16)
    b_hi = b.astype(BF16)
    b_lo = (b - b_hi.astype(F32)).astype(BF16)
    d = functools.partial(jnp.dot, preferred_element_type=F32)
    return d(a_hi, b_hi) + (d(a_hi, b_lo) + d(a_lo, b_hi))


def _unit_lower_solve(a, rhs, row, col):
    blk = (row // 16) == (col // 16)
    eye = (row == col).astype(F32)
    ad = jnp.where(blk, a, 0.0)
    ao = jnp.where(blk, 0.0, a)
    a2 = _dot_hp(ad, ad)
    a4 = _dot_hp(a2, a2)
    a8 = _dot_hp(a4, a4)
    p = eye - ad
    p = p + _dot_hp(p, a2)
    p = p + _dot_hp(p, a4)
    p = p + _dot_hp(p, a8)
    y = _dot_hp(p, rhs)
    bm = _dot_hp(p, ao)
    b2 = _dot_hp(bm, bm)
    z = y + _dot_hp(b2, y)
    return z - _dot_hp(bm, z)


def _gdn_kernel(qkv_ref, z_ref, ab_ref, convp_ref, s0_ref, cw_ref, alog_ref, dtb_ref, gnw_ref,
                og_ref, sfin_ref, convn_ref, xbuf_ref, s_ref, *, nc):
    c = pl.program_id(1)

    @pl.when(c == 0)
    def _():
        xbuf_ref[0:SUBLANES, :] = convp_ref[0]
        s_ref[...] = s0_ref[0]

    xbuf_ref[SUBLANES:SUBLANES + CHUNK, :] = qkv_ref[0]

    row = lax.broadcasted_iota(jnp.int32, (CHUNK, CHUNK), 0)
    col = lax.broadcasted_iota(jnp.int32, (CHUNK, CHUNK), 1)
    incl = row >= col
    strict = row > col

    ab = ab_ref[0]
    g_all = alog_ref[...] * jax.nn.softplus(ab + dtb_ref[...])
    tri = incl.astype(BF16)
    g_hi, g_mid, g_lo = _split3(g_all)
    d = functools.partial(jnp.dot, preferred_element_type=F32)
    gc_all = d(tri, g_hi) + (d(tri, g_mid) + d(tri, g_lo))
    gc_t = gc_all.T
    beta_all = jax.nn.sigmoid(ab)

    def conv_silu(col0):
        cs = slice(col0, col0 + HEAD)
        y = xbuf_ref[5:5 + CHUNK, cs] * cw_ref[0:1, cs]
        y = y + xbuf_ref[6:6 + CHUNK, cs] * cw_ref[1:2, cs]
        y = y + xbuf_ref[7:7 + CHUNK, cs] * cw_ref[2:3, cs]
        y = y + xbuf_ref[8:8 + CHUNK, cs] * cw_ref[3:4, cs]
        return y * jax.nn.sigmoid(y)

    for h in range(N_HEADS):
        qr = conv_silu(h * HEAD)
        kr = conv_silu(GROUP_W + h * HEAD)
        v = conv_silu(2 * GROUP_W + h * HEAD)
        qn = qr * lax.rsqrt(jnp.sum(qr * qr, axis=-1, keepdims=True) + 1e-6) * (HEAD ** -0.5)
        kn = kr * lax.rsqrt(jnp.sum(kr * kr, axis=-1, keepdims=True) + 1e-6)
        beta = beta_all[:, SUBLANES + h:SUBLANES + h + 1]
        gc_col = gc_all[:, h:h + 1]
        gc_row = gc_t[h:h + 1, :]
        g_last = gc_all[CHUNK - 1:CHUNK, h:h + 1]
        decay = jnp.where(incl, jnp.exp(jnp.where(incl, gc_col - gc_row, 0.0)), 0.0)
        kb = kn * beta
        a = jnp.where(strict, _bdot_nt(kb, kn) * decay, 0.0)
        qk = jnp.where(incl, _bdot_nt(qn, kn) * decay, 0.0)
        eg = jnp.exp(gc_col)
        rhs = jnp.concatenate([v * beta, kb * eg], axis=1)
        sol = _unit_lower_solve(a, rhs, row, col)
        u = sol[:, 0:HEAD]
        w = sol[:, HEAD:2 * HEAD]
        s_h = s_ref[h]
        v_new = u - _bdot(w, s_h)
        o = _bdot(qn * eg, s_h) + _bdot(qk, v_new)
        k_dec = kn * jnp.exp(g_last - gc_col)
        s_ref[h] = s_h * jnp.exp(g_last) + _bdot_tn(k_dec, v_new)
        on = o * lax.rsqrt(jnp.mean(o * o, axis=-1, keepdims=True) + EPS) * gnw_ref[...]
        zh = z_ref[0, :, h * HEAD:(h + 1) * HEAD]
        og_ref[0, :, h * HEAD:(h + 1) * HEAD] = (on * (zh * jax.nn.sigmoid(zh))).astype(BF16)

    xbuf_ref[0:SUBLANES, :] = xbuf_ref[CHUNK:CHUNK + SUBLANES, :]

    @pl.when(c == nc - 1)
    def _():
        sfin_ref[0] = s_ref[...]
        convn_ref[0] = xbuf_ref[CHUNK:CHUNK + SUBLANES, :]


def _gdn(proj3, conv_past8, s0, conv_w_t, neg_a, dt_bias, gnw):
    b, t = proj3.shape[0], proj3.shape[1]
    nc = t // CHUNK
    w3 = 3 * GROUP_W
    return pl.pallas_call(
        functools.partial(_gdn_kernel, nc=nc),
        out_shape=(jax.ShapeDtypeStruct((b, t, GROUP_W), BF16),
                   jax.ShapeDtypeStruct((b, N_HEADS, HEAD, HEAD), F32),
                   jax.ShapeDtypeStruct((b, SUBLANES, w3), F32)),
        grid=(b, nc),
        in_specs=[pl.BlockSpec((1, CHUNK, w3), lambda bi, c: (bi, c, COL_QKVG // w3)),
                  pl.BlockSpec((1, CHUNK, GROUP_W), lambda bi, c: (bi, c, COL_Z // GROUP_W)),
                  pl.BlockSpec((1, CHUNK, LANES), lambda bi, c: (bi, c, COL_AB // LANES)),
                  pl.BlockSpec((1, SUBLANES, w3), lambda bi, c: (bi, 0, 0)),
                  pl.BlockSpec((1, N_HEADS, HEAD, HEAD), lambda bi, c: (bi, 0, 0, 0)),
                  pl.BlockSpec((CONV_K, w3), lambda bi, c: (0, 0)),
                  pl.BlockSpec((1, LANES), lambda bi, c: (0, 0)),
                  pl.BlockSpec((1, LANES), lambda bi, c: (0, 0)),
                  pl.BlockSpec((1, HEAD), lambda bi, c: (0, 0))],
        out_specs=(pl.BlockSpec((1, CHUNK, GROUP_W), lambda bi, c: (bi, c, 0)),
                   pl.BlockSpec((1, N_HEADS, HEAD, HEAD), lambda bi, c: (bi, 0, 0, 0)),
                   pl.BlockSpec((1, SUBLANES, w3), lambda bi, c: (bi, 0, 0))),
        scratch_shapes=[pltpu.VMEM((CHUNK + SUBLANES, w3), F32),
                        pltpu.VMEM((N_HEADS, HEAD, HEAD), F32)],
        compiler_params=_cparams(("arbitrary", "arbitrary")),
        name="gdn",
    )(proj3, proj3, proj3, conv_past8, s0, conv_w_t, neg_a, dt_bias, gnw.reshape(1, HEAD))


def _outproj_kernel(od_ref, og_ref, x_ref, w1_ref, w2_ref, fnw_ref, rw_ref, rb_ref,
                    x1_ref, h2_ref, idx_ref, gate_ref):
    acc = jnp.dot(od_ref[...], w1_ref[...], preferred_element_type=F32)
    acc = acc + jnp.dot(og_ref[...], w2_ref[...], preferred_element_type=F32)
    x1 = x_ref[...] + acc
    x1_ref[...] = x1
    h = x1 * lax.rsqrt(jnp.mean(x1 * x1, axis=-1, keepdims=True) + EPS) * fnw_ref[...]
    h2_ref[...] = h
    logits = _dot_hp(h, rw_ref[...]) + rb_ref[...]
    lane = lax.broadcasted_iota(jnp.int32, logits.shape, 1)
    vals = logits
    idx_out = jnp.zeros(logits.shape, jnp.int32)
    top = []
    for k in range(TOP_K):
        mk = jnp.max(vals, axis=-1, keepdims=True)
        ik = jnp.min(jnp.where(vals == mk, lane, LANES), axis=-1, keepdims=True)
        top.append(mk)
        idx_out = jnp.where(lane == k, ik, idx_out)
        vals = jnp.where(lane == ik, -jnp.inf, vals)
    es = [jnp.exp(mk - top[0]) for mk in top]
    inv = 1.0 / (es[0] + es[1] + es[2] + es[3])
    gate_out = jnp.zeros(logits.shape, F32)
    for k in range(TOP_K):
        gate_out = jnp.where(lane == k, es[k] * inv, gate_out)
    idx_ref[...] = idx_out
    gate_ref[...] = gate_out


def _outproj(od, og, x2d, w1, w2, fnw, rw_pad, rb_pad):
    m = x2d.shape[0]
    tm = min(512, m)
    row = lambda i: (i, 0)
    const = lambda i: (0, 0)
    return pl.pallas_call(
        _outproj_kernel,
        out_shape=(jax.ShapeDtypeStruct((m, D_MODEL), F32),
                   jax.ShapeDtypeStruct((m, D_MODEL), F32),
                   jax.ShapeDtypeStruct((m, LANES), jnp.int32),
                   jax.ShapeDtypeStruct((m, LANES), F32)),
        grid=(m // tm,),
        in_specs=[pl.BlockSpec((tm, GROUP_W), row),
                  pl.BlockSpec((tm, GROUP_W), row),
                  pl.BlockSpec((tm, D_MODEL), row),
                  pl.BlockSpec((GROUP_W, D_MODEL), const),
                  pl.BlockSpec((GROUP_W, D_MODEL), const),
                  pl.BlockSpec((1, D_MODEL), const),
                  pl.BlockSpec((D_MODEL, LANES), const),
                  pl.BlockSpec((1, LANES), const)],
        out_specs=(pl.BlockSpec((tm, D_MODEL), row),
                   pl.BlockSpec((tm, D_MODEL), row),
                   pl.BlockSpec((tm, LANES), row),
                   pl.BlockSpec((tm, LANES), row)),
        compiler_params=_cparams(("arbitrary",)),
        name="outproj_router",
    )(od, og, x2d, w1, w2, fnw, rw_pad, rb_pad)


def _gather_kernel(idx_hbm, src_hbm, out_ref, idx_smem, isem, sem, *, rows):
    blk = pl.program_id(0)
    icopy = pltpu.make_async_copy(idx_hbm.at[blk], idx_smem, isem)
    icopy.start()
    icopy.wait()

    def issue(r, carry):
        tok = idx_smem[r]
        pltpu.make_async_copy(src_hbm.at[pl.ds(tok, 1)], out_ref.at[pl.ds(r, 1)], sem).start()
        return carry

    lax.fori_loop(0, rows, issue, 0)
    pltpu.make_async_copy(src_hbm.at[pl.ds(0, rows)], out_ref, sem).wait()


def _gather_rows(src, idx, rows):
    n_slots = idx.shape[0]
    d = src.shape[1]
    nb = n_slots // rows
    return pl.pallas_call(
        functools.partial(_gather_kernel, rows=rows),
        out_shape=jax.ShapeDtypeStruct((n_slots, d), src.dtype),
        grid=(nb,),
        in_specs=[pl.BlockSpec(memory_space=pl.ANY),
                  pl.BlockSpec(memory_space=pl.ANY)],
        out_specs=pl.BlockSpec((rows, d), lambda i: (i, 0)),
        scratch_shapes=[pltpu.SMEM((rows,), jnp.int32),
                        pltpu.SemaphoreType.DMA,
                        pltpu.SemaphoreType.DMA],
        compiler_params=_cparams(("arbitrary",)),
        name="moe_gather",
    )(idx.reshape(nb, rows), src)


def _ffn_kernel(be_ref, nu_ref, xs_ref, wg_ref, bg_ref, wu_ref, bu_ref, wd_ref, bd_ref, y_ref):
    blk = pl.program_id(0)
    f = pl.program_id(1)

    @pl.when(blk < nu_ref[0])
    def _():
        x = xs_ref[...].astype(BF16)
        gate = jnp.minimum(jnp.dot(x, wg_ref[0], preferred_element_type=F32) + bg_ref[0],
                           SWIGLU_LIMIT)
        up = jnp.clip(jnp.dot(x, wu_ref[0], preferred_element_type=F32) + bu_ref[0],
                      -SWIGLU_LIMIT, SWIGLU_LIMIT)
        act = (up + 1.0) * gate * jax.nn.sigmoid(SWIGLU_ALPHA * gate)
        part = jnp.dot(act.astype(BF16), wd_ref[0], preferred_element_type=F32)

        @pl.when(f == 0)
        def _():
            y_ref[...] = part + bd_ref[0]

        @pl.when(f > 0)
        def _():
            y_ref[...] += part

    @pl.when(jnp.logical_and(blk >= nu_ref[0], f == 0))
    def _():
        y_ref[...] = jnp.zeros(y_ref.shape, F32)


def _moe_ffn(block_e, n_used, xs, wg, bg, wu, bu, wd, bd):
    n_slots = xs.shape[0]
    nb = n_slots // EXPERT_BLOCK
    nf = D_FF // FFN_TF

    def blk_of(b, nu):
        return jnp.minimum(b, nu[0] - 1)

    def f_of(b, f, nu):
        return jnp.where(b < nu[0], f, nf - 1)

    grid_spec = pltpu.PrefetchScalarGridSpec(
        num_scalar_prefetch=2,
        grid=(nb, nf),
        in_specs=[pl.BlockSpec((EXPERT_BLOCK, D_MODEL), lambda b, f, be, nu: (blk_of(b, nu), 0)),
                  pl.BlockSpec((1, D_MODEL, FFN_TF), lambda b, f, be, nu: (be[b], 0, f_of(b, f, nu))),
                  pl.BlockSpec((1, 1, FFN_TF), lambda b, f, be, nu: (be[b], 0, f_of(b, f, nu))),
                  pl.BlockSpec((1, D_MODEL, FFN_TF), lambda b, f, be, nu: (be[b], 0, f_of(b, f, nu))),
                  pl.BlockSpec((1, 1, FFN_TF), lambda b, f, be, nu: (be[b], 0, f_of(b, f, nu))),
                  pl.BlockSpec((1, FFN_TF, D_MODEL), lambda b, f, be, nu: (be[b], f_of(b, f, nu), 0)),
                  pl.BlockSpec((1, 1, D_MODEL), lambda b, f, be, nu: (be[b], 0, 0))],
        out_specs=pl.BlockSpec((EXPERT_BLOCK, D_MODEL), lambda b, f, be, nu: (b, 0)),
    )
    return pl.pallas_call(
        _ffn_kernel,
        out_shape=jax.ShapeDtypeStruct((n_slots, D_MODEL), F32),
        grid_spec=grid_spec,
        compiler_params=_cparams(("arbitrary", "arbitrary")),
        name="moe_ffn",
    )(block_e, n_used, xs, wg, bg, wu, bu, wd, bd)


def _combine_kernel(pos_hbm, ys_hbm, x1_ref, gate_ref, fnw_ref, y_ref, pos_smem, ybuf_ref, isem, sem,
                    *, tm):
    blk = pl.program_id(0)
    icopy = pltpu.make_async_copy(pos_hbm.at[blk], pos_smem, isem)
    icopy.start()
    icopy.wait()

    def issue(t, carry):
        for k in range(TOP_K):
            p = pos_smem[t * TOP_K + k]
            pltpu.make_async_copy(ys_hbm.at[pl.ds(p, 1)], ybuf_ref.at[k, pl.ds(t, 1)], sem).start()
        return carry

    lax.fori_loop(0, tm, issue, 0)
    for k in range(TOP_K):
        pltpu.make_async_copy(ys_hbm.at[pl.ds(0, tm)], ybuf_ref.at[k], sem).wait()

    g = gate_ref[...]
    moe = g[:, 0:1] * ybuf_ref[0]
    for k in range(1, TOP_K):
        moe = moe + g[:, k:k + 1] * ybuf_ref[k]
    x2 = x1_ref[...] + moe
    y_ref[...] = x2 * lax.rsqrt(jnp.mean(x2 * x2, axis=-1, keepdims=True) + EPS) * fnw_ref[...]


def _combine(pos, y_slots, x1, gates, final_norm_w):
    m = x1.shape[0]
    tm = min(256, m)
    nb = m // tm
    row = lambda i: (i, 0)
    return pl.pallas_call(
        functools.partial(_combine_kernel, tm=tm),
        out_shape=jax.ShapeDtypeStruct((m, D_MODEL), F32),
        grid=(nb,),
        in_specs=[pl.BlockSpec(memory_space=pl.ANY),
                  pl.BlockSpec(memory_space=pl.ANY),
                  pl.BlockSpec((tm, D_MODEL), row),
                  pl.BlockSpec((tm, LANES), row),
                  pl.BlockSpec((1, D_MODEL), lambda i: (0, 0))],
        out_specs=pl.BlockSpec((tm, D_MODEL), row),
        scratch_shapes=[pltpu.SMEM((tm * TOP_K,), jnp.int32),
                        pltpu.VMEM((TOP_K, tm, D_MODEL), F32),
                        pltpu.SemaphoreType.DMA,
                        pltpu.SemaphoreType.DMA],
        compiler_params=_cparams(("arbitrary",)),
        name="moe_combine",
    )(pos.reshape(nb, tm * TOP_K), y_slots, x1, gates, final_norm_w.reshape(1, D_MODEL))


def _route(top_idx):
    n_tok = top_idx.shape[0]
    n_assign = n_tok * TOP_K
    flat_e = top_idx.reshape(n_assign)
    onehot = (flat_e[:, None] == jnp.arange(N_EXPERTS, dtype=jnp.int32)[None, :]).astype(jnp.int32)
    csum = jnp.cumsum(onehot, axis=0)
    rank = jnp.take_along_axis(csum, flat_e[:, None], axis=1)[:, 0] - 1
    counts = csum[-1]
    padded = (counts + EXPERT_BLOCK - 1) // EXPERT_BLOCK * EXPERT_BLOCK
    padded_end = jnp.cumsum(padded)
    padded_start = padded_end - padded
    dest = (padded_start[flat_e] + rank).astype(jnp.int32)
    n_blocks = -(-n_assign // EXPERT_BLOCK) + N_EXPERTS
    flat_tok = jnp.arange(n_assign, dtype=jnp.int32) // TOP_K
    slot_tok = jnp.zeros((n_blocks * EXPERT_BLOCK,), jnp.int32).at[dest].set(flat_tok)
    block_e = jnp.minimum(
        jnp.searchsorted(padded_end, jnp.arange(n_blocks, dtype=jnp.int32) * EXPERT_BLOCK, side='right'),
        N_EXPERTS - 1).astype(jnp.int32)
    n_used = (padded_end[-1] // EXPERT_BLOCK).astype(jnp.int32).reshape(1)
    return slot_tok, dest, block_e, n_used


def _mixer(x, pos, k_cache, v_cache, conv_past, s0, lam, attn_norm_w, w_perm, conv_w_t, neg_a, dt_b,
           diff_norm_w, gdn_norm_w):
    b, t, _ = x.shape
    m = b * t
    tm = min(1024, m)
    tabs = _rope_tables(pos)
    if t < tm:
        tabs = tuple(jnp.tile(a, (tm // t, 1)) for a in tabs)
    proj, qkv = _inproj(x.reshape(m, D_MODEL), attn_norm_w, w_perm, tabs)
    proj3 = proj.reshape(b, t, PROJ_COLS)
    qkv3 = qkv.reshape(b, t, 3 * GROUP_W)
    k_rows = proj3[:, :, GROUP_W:2 * GROUP_W].reshape(b, t, N_HEADS, HEAD)
    v_rows = proj3[:, :, 2 * GROUP_W:3 * GROUP_W].reshape(b, t, N_HEADS, HEAD)

    if k_cache is None:
        tq = tk = min(512, t)
        nkb = t // tk
        vt = qkv3[:, :, 2 * GROUP_W:].reshape(b, nkb, tk, N_HEADS, HEAD).transpose(0, 3, 1, 4, 2)
        o_d = _diff_attn(lam, qkv3, 0, qkv3, N_HEADS, vt, diff_norm_w, tq=tq, tk=tk, causal=True)
    else:
        past = k_cache.shape[1]
        k_all = jnp.concatenate([k_cache.reshape(b, past, GROUP_W).astype(BF16),
                                 qkv3[:, :, GROUP_W:2 * GROUP_W]], axis=1)
        v_all = jnp.concatenate([v_cache.reshape(b, past, N_HEADS, HEAD).astype(BF16),
                                 qkv3[:, :, 2 * GROUP_W:].reshape(b, t, N_HEADS, HEAD)], axis=1)
        vt = v_all.transpose(0, 2, 3, 1)[:, :, None]
        o_d = _diff_attn(lam, qkv3, 0, k_all, 0, vt, diff_norm_w, tq=t, tk=past + t, causal=False)

    conv_past8 = jnp.pad(conv_past, ((0, 0), (SUBLANES - (CONV_K - 1), 0), (0, 0)))
    o_g, s_new, conv8 = _gdn(proj3, conv_past8, s0, conv_w_t, neg_a, dt_b, gdn_norm_w)
    conv_new = conv8[:, SUBLANES - (CONV_K - 1):, :]
    return o_d.reshape(m, GROUP_W), o_g.reshape(m, GROUP_W), k_rows, v_rows, conv_new, s_new


def kernel(x_prompt, x_sample, cache_k_diff, cache_v_diff, state_conv, state_gdn, attn_norm_w, w_in,
           conv_w, a_log, dt_bias, lambda_q1, lambda_k1, lambda_q2, lambda_k2, diff_norm_w, gdn_norm_w,
           w_out, ffn_norm_w, router_w, router_b, w_gate, b_gate, w_up, b_up, w_down, b_down,
           final_norm_w):
    l = 0
    bp, tp, _ = x_prompt.shape
    bs, ts, _ = x_sample.shape
    past_len = cache_k_diff.shape[2]

    lam = (jnp.exp(jnp.sum(lambda_q1[l] * lambda_k1[l])) - jnp.exp(jnp.sum(lambda_q2[l] * lambda_k2[l]))
           + LAM_INIT).astype(F32).reshape(1, 1)

    wl = w_in[l]
    ncol = wl.shape[1]
    w_perm = jnp.concatenate(
        [wl[:, :COL_Z], wl[:, ncol - GROUP_W:], wl[:, COL_Z:COL_Z + 2 * N_HEADS],
         jnp.zeros((D_MODEL, PROJ_COLS - ncol), F32)], axis=1).astype(BF16)
    conv_w_t = conv_w[l].T
    pad8 = lambda a: jnp.pad(a.astype(F32), (0, LANES - N_HEADS)).reshape(1, LANES)
    neg_a = pad8(-jnp.exp(a_log[l]))
    dt_b = pad8(dt_bias[l])
    w1 = w_out[l][:GROUP_W].astype(BF16)
    w2 = w_out[l][GROUP_W:].astype(BF16)
    rw_pad = jnp.pad(router_w[l], ((0, 0), (0, LANES - N_EXPERTS)))
    rb_pad = jnp.concatenate([router_b[l].astype(F32),
                              jnp.full((LANES - N_EXPERTS,), -jnp.inf, F32)]).reshape(1, LANES)
    wg = w_gate[l].astype(BF16)
    wu = w_up[l].astype(BF16)
    wd = w_down[l].astype(BF16)
    bg = b_gate[l].reshape(N_EXPERTS, 1, D_FF)
    bu = b_up[l].reshape(N_EXPERTS, 1, D_FF)
    bd = b_down[l].reshape(N_EXPERTS, 1, D_MODEL)

    mix_args = (lam, attn_norm_w[l], w_perm, conv_w_t, neg_a, dt_b, diff_norm_w[l], gdn_norm_w[l])
    conv0 = jnp.zeros((bp, CONV_K - 1, 3 * GROUP_W), F32)
    s0 = jnp.zeros((bp, N_HEADS, HEAD, HEAD), F32)
    od_p, og_p, kp, vp, cp, sp = _mixer(x_prompt, jnp.arange(tp), None, None, conv0, s0, *mix_args)
    od_s, og_s, ks, vs, cs, ss = _mixer(x_sample, past_len + jnp.arange(ts), cache_k_diff[l],
                                        cache_v_diff[l], state_conv[l], state_gdn[l], *mix_args)

    fnw = ffn_norm_w[l].reshape(1, D_MODEL)
    x1_p, h2_p, idx_p, gate_p = _outproj(od_p, og_p, x_prompt.reshape(bp * tp, D_MODEL), w1, w2, fnw,
                                         rw_pad, rb_pad)
    x1_s, h2_s, idx_s, gate_s = _outproj(od_s, og_s, x_sample.reshape(bs * ts, D_MODEL), w1, w2, fnw,
                                         rw_pad, rb_pad)

    n_p = bp * tp
    h2 = jnp.concatenate([h2_p, h2_s], axis=0)
    top_idx = jnp.concatenate([idx_p[:, :TOP_K], idx_s[:, :TOP_K]], axis=0)
    slot_tok, dest, block_e, n_used = _route(top_idx)
    xs = _gather_rows(h2, slot_tok, EXPERT_BLOCK)
    y_slots = _moe_ffn(block_e, n_used, xs, wg, bg, wu, bu, wd, bd)
    y_p = _combine(dest[:n_p * TOP_K], y_slots, x1_p, gate_p, final_norm_w)
    y_s = _combine(dest[n_p * TOP_K:], y_slots, x1_s, gate_s, final_norm_w)

    return (y_p.reshape(bp, tp, D_MODEL), y_s.reshape(bs, ts, D_MODEL),
            kp[None], vp[None], cp[None], sp[None], ks[None], vs[None], cs[None], ss[None])
```

```python
import functools
import math

import jax
import jax.numpy as jnp
from jax import lax
from jax.experimental import pallas as pl
from jax.experimental.pallas import tpu as pltpu

F32 = jnp.float32
BF16 = jnp.bfloat16

D_MODEL = 2048
N_HEADS = 8
HEAD = 128
QK_DIM = 64
ROT_DIM = 16
ROT_HALF = ROT_DIM // 2
ROPE_THETA = 500000.0
CHUNK = 64
GROUP_W = N_HEADS * HEAD
CONV_K = 4
N_EXPERTS = 32
TOP_K = 4
D_FF = 2048
SWIGLU_LIMIT = 7.0
SWIGLU_ALPHA = 1.702
EPS = 1e-6
SUBLN_EPS = 1e-5
LAM_INIT = 0.8 - 0.6 * math.exp(-0.3 * 0)
Q_SCALE = QK_DIM ** -0.5 * math.log2(math.e)

LANES = 128
SUBLANES = 8
VMEM_LIMIT = 56 * 1024 * 1024

PROJ_TN = 512
PROJ_COLS = 7680
COL_QKVG = 3 * GROUP_W
COL_Z = 6 * GROUP_W
COL_AB = 7 * GROUP_W

EXPERT_BLOCK = 512
FFN_TF = 1024
COMBINE_TM = 256


def _cparams(sem, vmem=VMEM_LIMIT):
    return pltpu.CompilerParams(dimension_semantics=sem, vmem_limit_bytes=vmem)


def _inproj_kernel(x_ref, nw_ref, w_ref, c_ref, s1_ref, s2_ref, proj_ref, qkv_ref, h_ref):
    j = pl.program_id(1)

    @pl.when(j == 0)
    def _():
        x = x_ref[...]
        ms = jnp.mean(x * x, axis=-1, keepdims=True)
        h_ref[...] = (x * lax.rsqrt(ms + EPS) * nw_ref[...]).astype(BF16)

    acc = jnp.dot(h_ref[...], w_ref[...], preferred_element_type=F32)

    @pl.when(j < 4)
    def _():
        c = c_ref[...]
        s1 = s1_ref[...]
        s2 = s2_ref[...]
        scale = jnp.where(j < 2, Q_SCALE, 1.0).astype(F32)
        for g in range(PROJ_TN // LANES):
            xg = acc[:, g * LANES:(g + 1) * LANES]
            r = (xg * c + pltpu.roll(xg, LANES - ROT_HALF, 1) * s1
                 + pltpu.roll(xg, ROT_HALF, 1) * s2)
            proj_ref[:, g * LANES:(g + 1) * LANES] = r
            qkv_ref[:, g * LANES:(g + 1) * LANES] = (r * scale).astype(BF16)

    @pl.when(jnp.logical_and(j >= 4, j < 6))
    def _():
        proj_ref[...] = acc
        qkv_ref[...] = acc.astype(BF16)

    @pl.when(j >= 6)
    def _():
        proj_ref[...] = acc


def _rope_tables(pos):
    inv_freq = jnp.power(ROPE_THETA, -jnp.arange(ROT_HALF, dtype=F32) * (2.0 / ROT_DIM))
    ang = pos.astype(F32)[:, None] * inv_freq[None, :]
    cos, sin = jnp.cos(ang), jnp.sin(ang)
    t = pos.shape[0]
    one = jnp.ones((t, QK_DIM - ROT_DIM), F32)
    zero = jnp.zeros((t, QK_DIM - ROT_DIM), F32)
    z8 = jnp.zeros((t, ROT_HALF), F32)
    c = jnp.concatenate([cos, cos, one], axis=1)
    s1 = jnp.concatenate([-sin, z8, zero], axis=1)
    s2 = jnp.concatenate([z8, sin, zero], axis=1)
    return tuple(jnp.concatenate([a, a], axis=1) for a in (c, s1, s2))


def _inproj(x2d, norm_w, w_perm, tables):
    m = x2d.shape[0]
    tm = min(1024, m)
    tab_rows = tables[0].shape[0]
    n_tab = tab_rows // tm
    grid = (m // tm, PROJ_COLS // PROJ_TN)
    tab_spec = pl.BlockSpec((tm, LANES), lambda i, j: (i % n_tab, 0))
    return pl.pallas_call(
        _inproj_kernel,
        out_shape=(jax.ShapeDtypeStruct((m, PROJ_COLS), F32),
                   jax.ShapeDtypeStruct((m, 3 * GROUP_W), BF16)),
        grid=grid,
        in_specs=[pl.BlockSpec((tm, D_MODEL), lambda i, j: (i, 0)),
                  pl.BlockSpec((1, D_MODEL), lambda i, j: (0, 0)),
                  pl.BlockSpec((D_MODEL, PROJ_TN), lambda i, j: (0, j)),
                  tab_spec, tab_spec, tab_spec],
        out_specs=(pl.BlockSpec((tm, PROJ_TN), lambda i, j: (i, j)),
                   pl.BlockSpec((tm, PROJ_TN), lambda i, j: (i, jnp.minimum(j, 5)))),
        scratch_shapes=[pltpu.VMEM((tm, D_MODEL), BF16)],
        compiler_params=_cparams(("arbitrary", "arbitrary")),
        name="inproj",
    )(x2d, norm_w.reshape(1, D_MODEL), w_perm, *tables)


def _split_maps(q_ref, qq_ref, tq):
    q = q_ref[0]
    lane = lax.broadcasted_iota(jnp.int32, (tq, LANES), 1)
    zero = jnp.zeros_like(q)
    qq_ref[0:tq, :] = jnp.where(lane < QK_DIM, q, zero)
    qq_ref[tq:2 * tq, :] = jnp.where(lane >= QK_DIM, q, zero)


def _attn_finish(lam_ref, nw_ref, o_ref, l_ref, acc_ref, tq):
    lam = lam_ref[0, 0]
    inv_l = 1.0 / l_ref[...]
    o = acc_ref[...] * inv_l
    o = o[:, 0:tq] - lam * o[:, tq:2 * tq]
    ms = jnp.mean(o * o, axis=0, keepdims=True)
    on = (o * lax.rsqrt(ms + SUBLN_EPS)).T
    o_ref[0] = ((on * nw_ref[...]) * (1.0 - LAM_INIT)).astype(BF16)


def _attn_full_kernel(lam_ref, q_ref, k_ref, vt_ref, nw_ref, o_ref, qq_ref, l_ref, acc_ref, *, tq):
    _split_maps(q_ref, qq_ref, tq)
    s = lax.dot_general(k_ref[0], qq_ref[...], (((1,), (1,)), ((), ())),
                        preferred_element_type=F32)
    p = jnp.exp2(s - jnp.max(s, axis=0, keepdims=True))
    l_ref[...] = jnp.sum(p, axis=0, keepdims=True)
    acc_ref[...] = jnp.dot(vt_ref[0, 0, 0], p.astype(BF16), preferred_element_type=F32)
    _attn_finish(lam_ref, nw_ref, o_ref, l_ref, acc_ref, tq)


def _diff_attn_full(lam, q_src, k_all, vt, norm_w):
    b, tq = q_src.shape[0], q_src.shape[1]
    t_k = k_all.shape[1]
    return pl.pallas_call(
        functools.partial(_attn_full_kernel, tq=tq),
        out_shape=jax.ShapeDtypeStruct((b, tq, GROUP_W), BF16),
        grid=(b, N_HEADS),
        in_specs=[pl.BlockSpec(memory_space=pltpu.SMEM),
                  pl.BlockSpec((1, tq, HEAD), lambda bi, h: (bi, 0, h)),
                  pl.BlockSpec((1, t_k, HEAD), lambda bi, h: (bi, 0, h)),
                  pl.BlockSpec((1, 1, 1, HEAD, t_k), lambda bi, h: (bi, h, 0, 0, 0)),
                  pl.BlockSpec((1, HEAD), lambda bi, h: (0, 0))],
        out_specs=pl.BlockSpec((1, tq, HEAD), lambda bi, h: (bi, 0, h)),
        scratch_shapes=[pltpu.VMEM((2 * tq, HEAD), BF16),
                        pltpu.VMEM((1, 2 * tq), F32),
                        pltpu.VMEM((HEAD, 2 * tq), F32)],
        compiler_params=_cparams(("arbitrary", "arbitrary")),
        name="diff_attn_full",
    )(lam, q_src, k_all, vt, norm_w.reshape(1, HEAD))


def _attn_pipe_kernel(lam_ref, q_ref, k_ref, vt_ref, nw_ref, o_ref,
                      qq_ref, sa_ref, sb_ref, pa_ref, pb_ref, ala_ref, alb_ref, ma_ref, mb_ref, m_ref, l_ref,
                      acc_ref, *, tq, tk):
    i = pl.program_id(2)
    _split_maps(q_ref, qq_ref, tq)
    m_ref[...] = jnp.full(m_ref.shape, -jnp.inf, F32)
    l_ref[...] = jnp.zeros(l_ref.shape, F32)
    acc_ref[...] = jnp.zeros(acc_ref.shape, F32)
    s_bufs = (sa_ref, sb_ref)
    p_bufs = (pa_ref, pb_ref)
    al_bufs = (ala_ref, alb_ref)
    mx_bufs = (ma_ref, mb_ref)

    def scores(j, par, masked):
        kb = k_ref[0, pl.ds(pl.multiple_of(j * tk, tk), tk), :]
        s = lax.dot_general(kb, qq_ref[...], (((1,), (1,)), ((), ())), preferred_element_type=F32)
        if masked:
            kchunk = (j * tk + lax.broadcasted_iota(jnp.int32, (tk, 2 * tq), 0)) // CHUNK
            qcol = lax.broadcasted_iota(jnp.int32, (tk, 2 * tq), 1) & (tq - 1)
            s = jnp.where(kchunk <= (i * tq + qcol) // CHUNK, s, -jnp.inf)
        s_bufs[par][...] = s
        mx_bufs[par][...] = jnp.max(s, axis=0, keepdims=True)

    def softmax(par):
        s = s_bufs[par][...]
        m_old = m_ref[...]
        m_new = jnp.maximum(m_old, mx_bufs[par][...])
        alpha = jnp.exp2(m_old - m_new)
        p = jnp.exp2(s - m_new)
        l_ref[...] = alpha * l_ref[...] + jnp.sum(p, axis=0, keepdims=True)
        m_ref[...] = m_new
        p_bufs[par][...] = p.astype(BF16)
        al_bufs[par][...] = alpha

    def pv(j, par):
        upd = jnp.dot(vt_ref[0, 0, j], p_bufs[par][...], preferred_element_type=F32)
        acc_ref[...] = acc_ref[...] * al_bufs[par][...] + upd

    n = 2 * i + 2
    scores(0, 0, True)
    scores(1, 1, True)
    softmax(0)

    def pair(u, carry):
        t = 2 * u + 1
        scores(t + 1, 0, False)
        softmax(1)
        pv(t - 1, 0)
        scores(t + 2, 1, False)
        softmax(0)
        pv(t, 1)
        return carry

    lax.fori_loop(0, i - 1, pair, 0)

    @pl.when(i > 0)
    def _():
        t = n - 3
        scores(t + 1, 0, True)
        softmax(1)
        pv(t - 1, 0)
        scores(t + 2, 1, True)
        softmax(0)
        pv(t, 1)

    softmax(1)
    pv(n - 2, 0)
    pv(n - 1, 1)
    _attn_finish(lam_ref, nw_ref, o_ref, l_ref, acc_ref, tq)


def _diff_attn_prompt(lam, qkv3, vt, norm_w, *, tk):
    b, t = qkv3.shape[0], qkv3.shape[1]
    tq = 2 * tk
    nkb = t // tk
    kern = functools.partial(_attn_pipe_kernel, tq=tq, tk=tk)
    stat = pltpu.VMEM((1, 2 * tq), F32)
    return pl.pallas_call(
        kern,
        out_shape=jax.ShapeDtypeStruct((b, t, GROUP_W), BF16),
        grid=(b, N_HEADS, t // tq),
        in_specs=[pl.BlockSpec(memory_space=pltpu.SMEM),
                  pl.BlockSpec((1, tq, HEAD), lambda bi, h, i: (bi, i, h)),
                  pl.BlockSpec((1, t, HEAD), lambda bi, h, i: (bi, 0, N_HEADS + h)),
                  pl.BlockSpec((1, 1, nkb, HEAD, tk), lambda bi, h, i: (bi, h, 0, 0, 0)),
                  pl.BlockSpec((1, HEAD), lambda bi, h, i: (0, 0))],
        out_specs=pl.BlockSpec((1, tq, HEAD), lambda bi, h, i: (bi, i, h)),
        scratch_shapes=[pltpu.VMEM((2 * tq, HEAD), BF16),
                        pltpu.VMEM((tk, 2 * tq), F32),
                        pltpu.VMEM((tk, 2 * tq), F32),
                        pltpu.VMEM((tk, 2 * tq), BF16),
                        pltpu.VMEM((tk, 2 * tq), BF16),
                        stat, stat, stat, stat, stat, stat,
                        pltpu.VMEM((HEAD, 2 * tq), F32)],
        compiler_params=_cparams(("arbitrary", "arbitrary", "arbitrary")),
        name="diff_attn_prompt",
    )(lam, qkv3, qkv3, vt, norm_w.reshape(1, HEAD))


def _bdot(a, b):
    return jnp.dot(a.astype(BF16), b.astype(BF16), preferred_element_type=F32)


def _bdot_nt(a, b):
    return lax.dot_general(a.astype(BF16), b.astype(BF16), (((1,), (1,)), ((), ())),
                           preferred_element_type=F32)


def _bdot_tn(a, b):
    return lax.dot_general(a.astype(BF16), b.astype(BF16), (((0,), (0,)), ((), ())),
                           preferred_element_type=F32)


def _split3(a):
    hi = a.astype(BF16)
    r = a - hi.astype(F32)
    mid = r.astype(BF16)
    lo = (r - mid.astype(F32)).astype(BF16)
    return hi, mid, lo


def _dot_hp(a, b):
    a_hi = a.astype(BF16)
    a_lo = (a - a_hi.astype(F32)).astype(BF16)
    b_hi = b.astype(BF16)
    b_lo = (b - b_hi.astype(F32)).astype(BF16)
    d = functools.partial(jnp.dot, preferred_element_type=F32)
    return d(a_hi, b_hi) + (d(a_hi, b_lo) + d(a_lo, b_hi))


def _gdn_kernel(qkv_ref, z_ref, ab_ref, convp_ref, s0_ref, cw_ref, alog_ref, dtb_ref, gnw_ref,
                og_ref, sfin_ref, convn_ref, xbuf_ref, s_ref, *, nc):
    c = pl.program_id(1)

    @pl.when(c == 0)
    def _():
        xbuf_ref[0:SUBLANES, :] = convp_ref[0]
        s_ref[...] = s0_ref[0]

    xbuf_ref[SUBLANES:SUBLANES + CHUNK, :] = qkv_ref[0]

    row = lax.broadcasted_iota(jnp.int32, (CHUNK, CHUNK), 0)
    col = lax.broadcasted_iota(jnp.int32, (CHUNK, CHUNK), 1)
    incl = row >= col
    strict = row > col
    blk = (row // 16) == (col // 16)
    eye = (row == col).astype(F32)
    heads = range(N_HEADS)

    ab = ab_ref[0]
    g_all = alog_ref[...] * jax.nn.softplus(ab + dtb_ref[...])
    tri = incl.astype(BF16)
    g_hi, g_mid, g_lo = _split3(g_all)
    d = functools.partial(jnp.dot, preferred_element_type=F32)
    gc_all = d(tri, g_hi) + (d(tri, g_mid) + d(tri, g_lo))
    gc_t = gc_all.T
    beta_all = jax.nn.sigmoid(ab)

    def conv_silu(col0):
        cs = slice(col0, col0 + HEAD)
        y = xbuf_ref[5:5 + CHUNK, cs] * cw_ref[0:1, cs]
        y = y + xbuf_ref[6:6 + CHUNK, cs] * cw_ref[1:2, cs]
        y = y + xbuf_ref[7:7 + CHUNK, cs] * cw_ref[2:3, cs]
        y = y + xbuf_ref[8:8 + CHUNK, cs] * cw_ref[3:4, cs]
        return y * jax.nn.sigmoid(y)

    qr = [conv_silu(h * HEAD) for h in heads]
    kr = [conv_silu(GROUP_W + h * HEAD) for h in heads]
    v = [conv_silu(2 * GROUP_W + h * HEAD) for h in heads]
    qn = [x * lax.rsqrt(jnp.sum(x * x, axis=-1, keepdims=True) + 1e-6) * (HEAD ** -0.5) for x in qr]
    kn = [x * lax.rsqrt(jnp.sum(x * x, axis=-1, keepdims=True) + 1e-6) for x in kr]
    beta = [beta_all[:, SUBLANES + h:SUBLANES + h + 1] for h in heads]
    gc_col = [gc_all[:, h:h + 1] for h in heads]
    g_last = [gc_all[CHUNK - 1:CHUNK, h:h + 1] for h in heads]
    decay = [jnp.where(incl, jnp.exp(jnp.where(incl, gc_col[h] - gc_t[h:h + 1, :], 0.0)), 0.0)
             for h in heads]
    kb = [kn[h] * beta[h] for h in heads]
    a = [jnp.where(strict, _bdot_nt(kb[h], kn[h]) * decay[h], 0.0) for h in heads]
    qk = [jnp.where(incl, _bdot_nt(qn[h], kn[h]) * decay[h], 0.0) for h in heads]
    eg = [jnp.exp(gc_col[h]) for h in heads]
    rhs = [jnp.concatenate([v[h] * beta[h], kb[h] * eg[h]], axis=1) for h in heads]

    ad = [jnp.where(blk, x, 0.0) for x in a]
    ao = [jnp.where(blk, 0.0, x) for x in a]
    a2 = [_bdot(x, x) for x in ad]
    a4 = [_bdot(x, x) for x in a2]
    a8 = [_bdot(x, x) for x in a4]
    p = [eye - x for x in ad]
    p = [p[h] + _bdot(p[h], a2[h]) for h in heads]
    p = [p[h] + _bdot(p[h], a4[h]) for h in heads]
    p = [p[h] + _bdot(p[h], a8[h]) for h in heads]
    y = [_bdot(p[h], rhs[h]) for h in heads]
    bm = [_bdot(p[h], ao[h]) for h in heads]
    b2 = [_bdot(x, x) for x in bm]
    zz = [y[h] + _bdot(b2[h], y[h]) for h in heads]
    sol = [zz[h] - _bdot(bm[h], zz[h]) for h in heads]

    s_old = [s_ref[h] for h in heads]
    v_new = [sol[h][:, 0:HEAD] - _bdot(sol[h][:, HEAD:2 * HEAD], s_old[h]) for h in heads]
    o = [_bdot(qn[h] * eg[h], s_old[h]) + _bdot(qk[h], v_new[h]) for h in heads]
    k_dec = [kn[h] * jnp.exp(g_last[h] - gc_col[h]) for h in heads]
    for h in heads:
        s_ref[h] = s_old[h] * jnp.exp(g_last[h]) + _bdot_tn(k_dec[h], v_new[h])
    for h in heads:
        on = o[h] * lax.rsqrt(jnp.mean(o[h] * o[h], axis=-1, keepdims=True) + EPS) * gnw_ref[...]
        zh = z_ref[0, :, h * HEAD:(h + 1) * HEAD]
        og_ref[0, :, h * HEAD:(h + 1) * HEAD] = (on * (zh * jax.nn.sigmoid(zh))).astype(BF16)

    xbuf_ref[0:SUBLANES, :] = xbuf_ref[CHUNK:CHUNK + SUBLANES, :]

    @pl.when(c == nc - 1)
    def _():
        sfin_ref[0] = s_ref[...]
        convn_ref[0] = xbuf_ref[CHUNK:CHUNK + SUBLANES, :]


def _gdn(proj3, conv_past8, s0, conv_w_t, neg_a, dt_bias, gnw):
    b, t = proj3.shape[0], proj3.shape[1]
    nc = t // CHUNK
    w3 = 3 * GROUP_W
    return pl.pallas_call(
        functools.partial(_gdn_kernel, nc=nc),
        out_shape=(jax.ShapeDtypeStruct((b, t, GROUP_W), BF16),
                   jax.ShapeDtypeStruct((b, N_HEADS, HEAD, HEAD), F32),
                   jax.ShapeDtypeStruct((b, SUBLANES, w3), F32)),
        grid=(b, nc),
        in_specs=[pl.BlockSpec((1, CHUNK, w3), lambda bi, c: (bi, c, COL_QKVG // w3)),
                  pl.BlockSpec((1, CHUNK, GROUP_W), lambda bi, c: (bi, c, COL_Z // GROUP_W)),
                  pl.BlockSpec((1, CHUNK, LANES), lambda bi, c: (bi, c, COL_AB // LANES)),
                  pl.BlockSpec((1, SUBLANES, w3), lambda bi, c: (bi, 0, 0)),
                  pl.BlockSpec((1, N_HEADS, HEAD, HEAD), lambda bi, c: (bi, 0, 0, 0)),
                  pl.BlockSpec((CONV_K, w3), lambda bi, c: (0, 0)),
                  pl.BlockSpec((1, LANES), lambda bi, c: (0, 0)),
                  pl.BlockSpec((1, LANES), lambda bi, c: (0, 0)),
                  pl.BlockSpec((1, HEAD), lambda bi, c: (0, 0))],
        out_specs=(pl.BlockSpec((1, CHUNK, GROUP_W), lambda bi, c: (bi, c, 0)),
                   pl.BlockSpec((1, N_HEADS, HEAD, HEAD), lambda bi, c: (bi, 0, 0, 0)),
                   pl.BlockSpec((1, SUBLANES, w3), lambda bi, c: (bi, 0, 0))),
        scratch_shapes=[pltpu.VMEM((CHUNK + SUBLANES, w3), F32),
                        pltpu.VMEM((N_HEADS, HEAD, HEAD), F32)],
        compiler_params=_cparams(("arbitrary", "arbitrary")),
        name="gdn",
    )(proj3, proj3, proj3, conv_past8, s0, conv_w_t, neg_a, dt_bias, gnw.reshape(1, HEAD))


def _outproj_kernel(od_ref, og_ref, x_ref, w1_ref, w2_ref, fnw_ref, rw_ref, rb_ref,
                    x1_ref, h2_ref, idx_ref, gate_ref):
    acc = jnp.dot(od_ref[...], w1_ref[...], preferred_element_type=F32)
    acc = acc + jnp.dot(og_ref[...], w2_ref[...], preferred_element_type=F32)
    x1 = x_ref[...] + acc
    x1_ref[...] = x1
    h = x1 * lax.rsqrt(jnp.mean(x1 * x1, axis=-1, keepdims=True) + EPS) * fnw_ref[...]
    h2_ref[...] = h
    logits = _dot_hp(h, rw_ref[...]) + rb_ref[...]
    lane = lax.broadcasted_iota(jnp.int32, logits.shape, 1)
    vals = logits
    idx_out = jnp.zeros(logits.shape, jnp.int32)
    top = []
    for k in range(TOP_K):
        mk = jnp.max(vals, axis=-1, keepdims=True)
        ik = jnp.min(jnp.where(vals == mk, lane, LANES), axis=-1, keepdims=True)
        top.append(mk)
        idx_out = jnp.where(lane == k, ik, idx_out)
        vals = jnp.where(lane == ik, -jnp.inf, vals)
    es = [jnp.exp(mk - top[0]) for mk in top]
    inv = 1.0 / (es[0] + es[1] + es[2] + es[3])
    gate_out = jnp.zeros(logits.shape, F32)
    for k in range(TOP_K):
        gate_out = jnp.where(lane == k, es[k] * inv, gate_out)
    idx_ref[...] = idx_out
    gate_ref[...] = gate_out


def _outproj(od, og, x2d, w1, w2, fnw, rw_pad, rb_pad):
    m = x2d.shape[0]
    tm = min(512, m)
    row = lambda i: (i, 0)
    const = lambda i: (0, 0)
    return pl.pallas_call(
        _outproj_kernel,
        out_shape=(jax.ShapeDtypeStruct((m, D_MODEL), F32),
                   jax.ShapeDtypeStruct((m, D_MODEL), F32),
                   jax.ShapeDtypeStruct((m, LANES), jnp.int32),
                   jax.ShapeDtypeStruct((m, LANES), F32)),
        grid=(m // tm,),
        in_specs=[pl.BlockSpec((tm, GROUP_W), row),
                  pl.BlockSpec((tm, GROUP_W), row),
                  pl.BlockSpec((tm, D_MODEL), row),
                  pl.BlockSpec((GROUP_W, D_MODEL), const),
                  pl.BlockSpec((GROUP_W, D_MODEL), const),
                  pl.BlockSpec((1, D_MODEL), const),
                  pl.BlockSpec((D_MODEL, LANES), const),
                  pl.BlockSpec((1, LANES), const)],
        out_specs=(pl.BlockSpec((tm, D_MODEL), row),
                   pl.BlockSpec((tm, D_MODEL), row),
                   pl.BlockSpec((tm, LANES), row),
                   pl.BlockSpec((tm, LANES), row)),
        compiler_params=_cparams(("arbitrary",)),
        name="outproj_router",
    )(od, og, x2d, w1, w2, fnw, rw_pad, rb_pad)


def _ffn_kernel(be_ref, nu_ref, tok_hbm, h2_hbm, wg_ref, bg_ref, wu_ref, bu_ref, wd_ref, bd_ref, y_ref,
                xbuf_ref, xbf_ref, idx_smem, isem, rsem, *, nf):
    blk = pl.program_id(0)
    f = pl.program_id(1)
    nu = nu_ref[0]
    slot = blk % 2
    nslot = 1 - slot

    def idx_copy(b, s):
        return pltpu.make_async_copy(tok_hbm.at[b], idx_smem.at[s], isem.at[s])

    def issue_rows(s):
        def body(r, carry):
            tok = idx_smem[s, r]
            pltpu.make_async_copy(h2_hbm.at[pl.ds(tok, 1)], xbuf_ref.at[s, pl.ds(r, 1)],
                                  rsem.at[s]).start()
            return carry

        lax.fori_loop(0, EXPERT_BLOCK, body, 0)

    def wait_rows(s):
        pltpu.make_async_copy(h2_hbm.at[pl.ds(0, EXPERT_BLOCK)], xbuf_ref.at[s], rsem.at[s]).wait()

    @pl.when(jnp.logical_and(f == 0, blk == 0))
    def _():
        first = idx_copy(0, 0)
        first.start()
        first.wait()
        issue_rows(0)

        @pl.when(1 < nu)
        def _():
            idx_copy(1, 1).start()

    @pl.when(jnp.logical_and(f == 0, blk < nu))
    def _():
        @pl.when(blk + 1 < nu)
        def _():
            idx_copy(blk + 1, nslot).wait()
            issue_rows(nslot)

        wait_rows(slot)
        xbf_ref[...] = xbuf_ref[slot].astype(BF16)

    @pl.when(jnp.logical_and(f == nf - 1, blk + 2 < nu))
    def _():
        idx_copy(blk + 2, slot).start()

    @pl.when(blk < nu)
    def _():
        x = xbf_ref[...]
        gate = jnp.minimum(jnp.dot(x, wg_ref[0], preferred_element_type=F32) + bg_ref[0],
                           SWIGLU_LIMIT)
        up = jnp.clip(jnp.dot(x, wu_ref[0], preferred_element_type=F32) + bu_ref[0],
                      -SWIGLU_LIMIT, SWIGLU_LIMIT)
        act = (up + 1.0) * gate * jax.nn.sigmoid(SWIGLU_ALPHA * gate)
        part = jnp.dot(act.astype(BF16), wd_ref[0], preferred_element_type=F32)

        @pl.when(f == 0)
        def _():
            y_ref[...] = part + bd_ref[0]

        @pl.when(f > 0)
        def _():
            y_ref[...] += part

    @pl.when(jnp.logical_and(blk >= nu, f == 0))
    def _():
        y_ref[...] = jnp.zeros(y_ref.shape, F32)


def _moe_ffn(block_e, n_used, slot_tok, h2, wg, bg, wu, bu, wd, bd):
    n_slots = slot_tok.shape[0]
    nb = n_slots // EXPERT_BLOCK
    nf = D_FF // FFN_TF

    def f_of(b, f, nu):
        return jnp.where(b < nu[0], f, nf - 1)

    grid_spec = pltpu.PrefetchScalarGridSpec(
        num_scalar_prefetch=2,
        grid=(nb, nf),
        in_specs=[pl.BlockSpec(memory_space=pl.ANY),
                  pl.BlockSpec(memory_space=pl.ANY),
                  pl.BlockSpec((1, D_MODEL, FFN_TF), lambda b, f, be, nu: (be[b], 0, f_of(b, f, nu))),
                  pl.BlockSpec((1, 1, FFN_TF), lambda b, f, be, nu: (be[b], 0, f_of(b, f, nu))),
                  pl.BlockSpec((1, D_MODEL, FFN_TF), lambda b, f, be, nu: (be[b], 0, f_of(b, f, nu))),
                  pl.BlockSpec((1, 1, FFN_TF), lambda b, f, be, nu: (be[b], 0, f_of(b, f, nu))),
                  pl.BlockSpec((1, FFN_TF, D_MODEL), lambda b, f, be, nu: (be[b], f_of(b, f, nu), 0)),
                  pl.BlockSpec((1, 1, D_MODEL), lambda b, f, be, nu: (be[b], 0, 0))],
        out_specs=pl.BlockSpec((EXPERT_BLOCK, D_MODEL), lambda b, f, be, nu: (b, 0)),
        scratch_shapes=[pltpu.VMEM((2, EXPERT_BLOCK, D_MODEL), F32),
                        pltpu.VMEM((EXPERT_BLOCK, D_MODEL), BF16),
                        pltpu.SMEM((2, EXPERT_BLOCK), jnp.int32),
                        pltpu.SemaphoreType.DMA((2,)),
                        pltpu.SemaphoreType.DMA((2,))],
    )
    return pl.pallas_call(
        functools.partial(_ffn_kernel, nf=nf),
        out_shape=jax.ShapeDtypeStruct((n_slots, D_MODEL), F32),
        grid_spec=grid_spec,
        compiler_params=_cparams(("arbitrary", "arbitrary")),
        name="moe_ffn",
    )(block_e, n_used, slot_tok.reshape(nb, EXPERT_BLOCK), h2, wg, bg, wu, bu, wd, bd)


def _combine_kernel(pos_hbm, ys_hbm, x1_ref, gate_ref, fnw_ref, y_ref, pos_smem, ybuf_ref, isem, rsem,
                    *, tm, nb):
    blk = pl.program_id(0)
    slot = blk % 2
    nslot = 1 - slot

    def idx_copy(b, s):
        return pltpu.make_async_copy(pos_hbm.at[b], pos_smem.at[s], isem.at[s])

    def issue_rows(s):
        def body(t, carry):
            for k in range(TOP_K):
                p = pos_smem[s, t * TOP_K + k]
                pltpu.make_async_copy(ys_hbm.at[pl.ds(p, 1)], ybuf_ref.at[s, k, pl.ds(t, 1)],
                                      rsem.at[s]).start()
            return carry

        lax.fori_loop(0, tm, body, 0)

    def wait_rows(s):
        for k in range(TOP_K):
            pltpu.make_async_copy(ys_hbm.at[pl.ds(0, tm)], ybuf_ref.at[s, k], rsem.at[s]).wait()

    @pl.when(blk == 0)
    def _():
        first = idx_copy(0, 0)
        first.start()
        first.wait()
        issue_rows(0)
        if nb > 1:
            idx_copy(1, 1).start()

    @pl.when(blk + 1 < nb)
    def _():
        idx_copy(blk + 1, nslot).wait()
        issue_rows(nslot)

    @pl.when(blk + 2 < nb)
    def _():
        idx_copy(blk + 2, slot).start()

    wait_rows(slot)
    g = gate_ref[...]
    moe = g[:, 0:1] * ybuf_ref[slot, 0]
    for k in range(1, TOP_K):
        moe = moe + g[:, k:k + 1] * ybuf_ref[slot, k]
    x2 = x1_ref[...] + moe
    y_ref[...] = x2 * lax.rsqrt(jnp.mean(x2 * x2, axis=-1, keepdims=True) + EPS) * fnw_ref[...]


def _combine(pos, y_slots, x1, gates, final_norm_w):
    m = x1.shape[0]
    tm = min(COMBINE_TM, m)
    nb = m // tm
    row = lambda i: (i, 0)
    return pl.pallas_call(
        functools.partial(_combine_kernel, tm=tm, nb=nb),
        out_shape=jax.ShapeDtypeStruct((m, D_MODEL), F32),
        grid=(nb,),
        in_specs=[pl.BlockSpec(memory_space=pl.ANY),
                  pl.BlockSpec(memory_space=pl.ANY),
                  pl.BlockSpec((tm, D_MODEL), row),
                  pl.BlockSpec((tm, LANES), row),
                  pl.BlockSpec((1, D_MODEL), lambda i: (0, 0))],
        out_specs=pl.BlockSpec((tm, D_MODEL), row),
        scratch_shapes=[pltpu.SMEM((2, tm * TOP_K), jnp.int32),
                        pltpu.VMEM((2, TOP_K, tm, D_MODEL), F32),
                        pltpu.SemaphoreType.DMA((2,)),
                        pltpu.SemaphoreType.DMA((2,))],
        compiler_params=_cparams(("arbitrary",)),
        name="moe_combine",
    )(pos.reshape(nb, tm * TOP_K), y_slots, x1, gates, final_norm_w.reshape(1, D_MODEL))


def _route(top_idx):
    n_tok = top_idx.shape[0]
    n_assign = n_tok * TOP_K
    flat_e = top_idx.reshape(n_assign)
    onehot = (flat_e[:, None] == jnp.arange(N_EXPERTS, dtype=jnp.int32)[None, :]).astype(jnp.int32)
    csum = jnp.cumsum(onehot, axis=0)
    rank = jnp.sum(csum * onehot, axis=1) - 1
    counts = csum[-1]
    padded = (counts + EXPERT_BLOCK - 1) // EXPERT_BLOCK * EXPERT_BLOCK
    padded_end = jnp.cumsum(padded)
    padded_start = padded_end - padded
    dest = (jnp.sum(onehot * padded_start[None, :], axis=1) + rank).astype(jnp.int32)
    n_blocks = -(-n_assign // EXPERT_BLOCK) + N_EXPERTS
    flat_tok = jnp.arange(n_assign, dtype=jnp.int32) // TOP_K
    slot_tok = jnp.zeros((n_blocks * EXPERT_BLOCK,), jnp.int32).at[dest].set(flat_tok)
    starts = jnp.arange(n_blocks, dtype=jnp.int32) * EXPERT_BLOCK
    block_e = jnp.minimum(jnp.sum((padded_end[None, :] <= starts[:, None]).astype(jnp.int32), axis=1),
                          N_EXPERTS - 1).astype(jnp.int32)
    n_used = (padded_end[-1] // EXPERT_BLOCK).astype(jnp.int32).reshape(1)
    return slot_tok, dest, block_e, n_used


def _mixer(x, pos, k_cache, v_cache, conv_past, s0, lam, attn_norm_w, w_perm, conv_w_t, neg_a, dt_b,
           diff_norm_w, gdn_norm_w):
    b, t, _ = x.shape
    m = b * t
    tm = min(1024, m)
    tabs = _rope_tables(pos)
    if t < tm:
        tabs = tuple(jnp.tile(a, (tm // t, 1)) for a in tabs)
    proj, qkv = _inproj(x.reshape(m, D_MODEL), attn_norm_w, w_perm, tabs)
    proj3 = proj.reshape(b, t, PROJ_COLS)
    qkv3 = qkv.reshape(b, t, 3 * GROUP_W)
    k_rows = proj3[:, :, GROUP_W:2 * GROUP_W].reshape(b, t, N_HEADS, HEAD)
    v_rows = proj3[:, :, 2 * GROUP_W:3 * GROUP_W].reshape(b, t, N_HEADS, HEAD)

    if k_cache is None:
        tk = min(512, t // 4)
        vt = qkv3[:, :, 2 * GROUP_W:].reshape(b, t // tk, tk, N_HEADS, HEAD).transpose(0, 3, 1, 4, 2)
        o_d = _diff_attn_prompt(lam, qkv3, vt, diff_norm_w, tk=tk)
    else:
        past = k_cache.shape[1]
        k_all = jnp.concatenate([k_cache.reshape(b, past, GROUP_W).astype(BF16),
                                 qkv3[:, :, GROUP_W:2 * GROUP_W]], axis=1)
        v_all = jnp.concatenate([v_cache.reshape(b, past, N_HEADS, HEAD).astype(BF16),
                                 qkv3[:, :, 2 * GROUP_W:].reshape(b, t, N_HEADS, HEAD)], axis=1)
        vt = v_all.transpose(0, 2, 3, 1)[:, :, None]
        o_d = _diff_attn_full(lam, qkv3, k_all, vt, diff_norm_w)

    conv_past8 = jnp.pad(conv_past, ((0, 0), (SUBLANES - (CONV_K - 1), 0), (0, 0)))
    o_g, s_new, conv8 = _gdn(proj3, conv_past8, s0, conv_w_t, neg_a, dt_b, gdn_norm_w)
    conv_new = conv8[:, SUBLANES - (CONV_K - 1):, :]
    return o_d.reshape(m, GROUP_W), o_g.reshape(m, GROUP_W), k_rows, v_rows, conv_new, s_new


def kernel(x_prompt, x_sample, cache_k_diff, cache_v_diff, state_conv, state_gdn, attn_norm_w, w_in,
           conv_w, a_log, dt_bias, lambda_q1, lambda_k1, lambda_q2, lambda_k2, diff_norm_w, gdn_norm_w,
           w_out, ffn_norm_w, router_w, router_b, w_gate, b_gate, w_up, b_up, w_down, b_down,
           final_norm_w):
    l = 0
    bp, tp, _ = x_prompt.shape
    bs, ts, _ = x_sample.shape
    past_len = cache_k_diff.shape[2]

    lam = (jnp.exp(jnp.sum(lambda_q1[l] * lambda_k1[l])) - jnp.exp(jnp.sum(lambda_q2[l] * lambda_k2[l]))
           + LAM_INIT).astype(F32).reshape(1, 1)

    wl = w_in[l]
    ncol = wl.shape[1]
    w_perm = jnp.concatenate(
        [wl[:, :COL_Z], wl[:, ncol - GROUP_W:], wl[:, COL_Z:COL_Z + 2 * N_HEADS],
         jnp.zeros((D_MODEL, PROJ_COLS - ncol), F32)], axis=1).astype(BF16)
    conv_w_t = conv_w[l].T
    pad8 = lambda a: jnp.pad(a.astype(F32), (0, LANES - N_HEADS)).reshape(1, LANES)
    neg_a = pad8(-jnp.exp(a_log[l]))
    dt_b = pad8(dt_bias[l])
    w1 = w_out[l][:GROUP_W].astype(BF16)
    w2 = w_out[l][GROUP_W:].astype(BF16)
    rw_pad = jnp.pad(router_w[l], ((0, 0), (0, LANES - N_EXPERTS)))
    rb_pad = jnp.concatenate([router_b[l].astype(F32),
                              jnp.full((LANES - N_EXPERTS,), -jnp.inf, F32)]).reshape(1, LANES)
    wg = w_gate[l].astype(BF16)
    wu = w_up[l].astype(BF16)
    wd = w_down[l].astype(BF16)
    bg = b_gate[l].reshape(N_EXPERTS, 1, D_FF)
    bu = b_up[l].reshape(N_EXPERTS, 1, D_FF)
    bd = b_down[l].reshape(N_EXPERTS, 1, D_MODEL)

    mix_args = (lam, attn_norm_w[l], w_perm, conv_w_t, neg_a, dt_b, diff_norm_w[l], gdn_norm_w[l])
    conv0 = jnp.zeros((bp, CONV_K - 1, 3 * GROUP_W), F32)
    s0 = jnp.zeros((bp, N_HEADS, HEAD, HEAD), F32)
    od_p, og_p, kp, vp, cp, sp = _mixer(x_prompt, jnp.arange(tp), None, None, conv0, s0, *mix_args)
    od_s, og_s, ks, vs, cs, ss = _mixer(x_sample, past_len + jnp.arange(ts), cache_k_diff[l],
                                        cache_v_diff[l], state_conv[l], state_gdn[l], *mix_args)

    fnw = ffn_norm_w[l].reshape(1, D_MODEL)
    x1_p, h2_p, idx_p, gate_p = _outproj(od_p, og_p, x_prompt.reshape(bp * tp, D_MODEL), w1, w2, fnw,
                                         rw_pad, rb_pad)
    x1_s, h2_s, idx_s, gate_s = _outproj(od_s, og_s, x_sample.reshape(bs * ts, D_MODEL), w1, w2, fnw,
                                         rw_pad, rb_pad)

    n_p = bp * tp
    h2 = jnp.concatenate([h2_p, h2_s], axis=0)
    top_idx = jnp.concatenate([idx_p[:, :TOP_K], idx_s[:, :TOP_K]], axis=0)
    slot_tok, dest, block_e, n_used = _route(top_idx)
    y_slots = _moe_ffn(block_e, n_used, slot_tok, h2, wg, bg, wu, bu, wd, bd)
    y_p = _combine(dest[:n_p * TOP_K], y_slots, x1_p, gate_p, final_norm_w)
    y_s = _combine(dest[n_p * TOP_K:], y_slots, x1_s, gate_s, final_norm_w)

    return (y_p.reshape(bp, tp, D_MODEL), y_s.reshape(bs, ts, D_MODEL),
            kp[None], vp[None], cp[None], sp[None], ks[None], vs[None], cs[None], ss[None])
```

```python
import functools
import math

import jax
import jax.numpy as jnp
from jax import lax
from jax.experimental import pallas as pl
from jax.experimental.pallas import tpu as pltpu

F32 = jnp.float32
BF16 = jnp.bfloat16

D_MODEL = 2048
N_HEADS = 8
HEAD = 128
QK_DIM = 64
ROT_DIM = 16
ROT_HALF = ROT_DIM // 2
ROPE_THETA = 500000.0
CHUNK = 64
GROUP_W = N_HEADS * HEAD
CONV_K = 4
N_EXPERTS = 32
TOP_K = 4
D_FF = 2048
SWIGLU_LIMIT = 7.0
SWIGLU_ALPHA = 1.702
EPS = 1e-6
SUBLN_EPS = 1e-5
LAM_INIT = 0.8 - 0.6 * math.exp(-0.3 * 0)
Q_SCALE = QK_DIM ** -0.5 * math.log2(math.e)

LANES = 128
SUBLANES = 8
VMEM_LIMIT = 56 * 1024 * 1024

PROJ_TN = 512
PROJ_COLS = 7680
COL_QKVG = 3 * GROUP_W
COL_Z = 6 * GROUP_W
COL_AB = 7 * GROUP_W

EXPERT_BLOCK = 512
FFN_TF = 1024
COMBINE_TM = 256


def _cparams(sem, vmem=VMEM_LIMIT):
    return pltpu.CompilerParams(dimension_semantics=sem, vmem_limit_bytes=vmem)


def _inproj_kernel(x_ref, nw_ref, w_ref, c_ref, s1_ref, s2_ref, proj_ref, qkv_ref, h_ref):
    j = pl.program_id(1)

    @pl.when(j == 0)
    def _():
        x = x_ref[...]
        ms = jnp.mean(x * x, axis=-1, keepdims=True)
        h_ref[...] = (x * lax.rsqrt(ms + EPS) * nw_ref[...]).astype(BF16)

    acc = jnp.dot(h_ref[...], w_ref[...], preferred_element_type=F32)

    @pl.when(j < 4)
    def _():
        c = c_ref[...]
        s1 = s1_ref[...]
        s2 = s2_ref[...]
        scale = jnp.where(j < 2, Q_SCALE, 1.0).astype(F32)
        for g in range(PROJ_TN // LANES):
            xg = acc[:, g * LANES:(g + 1) * LANES]
            r = (xg * c + pltpu.roll(xg, LANES - ROT_HALF, 1) * s1
                 + pltpu.roll(xg, ROT_HALF, 1) * s2)
            proj_ref[:, g * LANES:(g + 1) * LANES] = r
            qkv_ref[:, g * LANES:(g + 1) * LANES] = (r * scale).astype(BF16)

    @pl.when(jnp.logical_and(j >= 4, j < 6))
    def _():
        proj_ref[...] = acc
        qkv_ref[...] = acc.astype(BF16)

    @pl.when(j >= 6)
    def _():
        proj_ref[...] = acc


def _rope_tables(pos):
    inv_freq = jnp.power(ROPE_THETA, -jnp.arange(ROT_HALF, dtype=F32) * (2.0 / ROT_DIM))
    ang = pos.astype(F32)[:, None] * inv_freq[None, :]
    cos, sin = jnp.cos(ang), jnp.sin(ang)
    t = pos.shape[0]
    one = jnp.ones((t, QK_DIM - ROT_DIM), F32)
    zero = jnp.zeros((t, QK_DIM - ROT_DIM), F32)
    z8 = jnp.zeros((t, ROT_HALF), F32)
    c = jnp.concatenate([cos, cos, one], axis=1)
    s1 = jnp.concatenate([-sin, z8, zero], axis=1)
    s2 = jnp.concatenate([z8, sin, zero], axis=1)
    return tuple(jnp.concatenate([a, a], axis=1) for a in (c, s1, s2))


def _inproj(x2d, norm_w, w_perm, tables):
    m = x2d.shape[0]
    tm = min(1024, m)
    tab_rows = tables[0].shape[0]
    n_tab = tab_rows // tm
    grid = (m // tm, PROJ_COLS // PROJ_TN)
    tab_spec = pl.BlockSpec((tm, LANES), lambda i, j: (i % n_tab, 0))
    return pl.pallas_call(
        _inproj_kernel,
        out_shape=(jax.ShapeDtypeStruct((m, PROJ_COLS), F32),
                   jax.ShapeDtypeStruct((m, 3 * GROUP_W), BF16)),
        grid=grid,
        in_specs=[pl.BlockSpec((tm, D_MODEL), lambda i, j: (i, 0)),
                  pl.BlockSpec((1, D_MODEL), lambda i, j: (0, 0)),
                  pl.BlockSpec((D_MODEL, PROJ_TN), lambda i, j: (0, j)),
                  tab_spec, tab_spec, tab_spec],
        out_specs=(pl.BlockSpec((tm, PROJ_TN), lambda i, j: (i, j)),
                   pl.BlockSpec((tm, PROJ_TN), lambda i, j: (i, jnp.minimum(j, 5)))),
        scratch_shapes=[pltpu.VMEM((tm, D_MODEL), BF16)],
        compiler_params=_cparams(("arbitrary", "arbitrary")),
        name="inproj",
    )(x2d, norm_w.reshape(1, D_MODEL), w_perm, *tables)


def _split_maps(q_ref, qq_ref, tq):
    q = q_ref[0]
    lane = lax.broadcasted_iota(jnp.int32, (tq, LANES), 1)
    zero = jnp.zeros_like(q)
    qq_ref[0:tq, :] = jnp.where(lane < QK_DIM, q, zero)
    qq_ref[tq:2 * tq, :] = jnp.where(lane >= QK_DIM, q, zero)


def _attn_finish(lam_ref, nw_ref, o_ref, l_ref, acc_ref, tq):
    lam = lam_ref[0, 0]
    inv_l = 1.0 / l_ref[...]
    o = acc_ref[...] * inv_l
    o = o[:, 0:tq] - lam * o[:, tq:2 * tq]
    ms = jnp.mean(o * o, axis=0, keepdims=True)
    on = (o * lax.rsqrt(ms + SUBLN_EPS)).T
    o_ref[0] = ((on * nw_ref[...]) * (1.0 - LAM_INIT)).astype(BF16)


def _attn_full_kernel(lam_ref, q_ref, k_ref, vt_ref, nw_ref, o_ref, qq_ref, l_ref, acc_ref, *, tq):
    _split_maps(q_ref, qq_ref, tq)
    s = lax.dot_general(k_ref[0], qq_ref[...], (((1,), (1,)), ((), ())),
                        preferred_element_type=F32)
    p = jnp.exp2(s - jnp.max(s, axis=0, keepdims=True))
    l_ref[...] = jnp.sum(p, axis=0, keepdims=True)
    acc_ref[...] = jnp.dot(vt_ref[0, 0, 0], p.astype(BF16), preferred_element_type=F32)
    _attn_finish(lam_ref, nw_ref, o_ref, l_ref, acc_ref, tq)


def _diff_attn_full(lam, q_src, k_all, vt, norm_w):
    b, tq = q_src.shape[0], q_src.shape[1]
    t_k = k_all.shape[1]
    return pl.pallas_call(
        functools.partial(_attn_full_kernel, tq=tq),
        out_shape=jax.ShapeDtypeStruct((b, tq, GROUP_W), BF16),
        grid=(b, N_HEADS),
        in_specs=[pl.BlockSpec(memory_space=pltpu.SMEM),
                  pl.BlockSpec((1, tq, HEAD), lambda bi, h: (bi, 0, h)),
                  pl.BlockSpec((1, t_k, HEAD), lambda bi, h: (bi, 0, h)),
                  pl.BlockSpec((1, 1, 1, HEAD, t_k), lambda bi, h: (bi, h, 0, 0, 0)),
                  pl.BlockSpec((1, HEAD), lambda bi, h: (0, 0))],
        out_specs=pl.BlockSpec((1, tq, HEAD), lambda bi, h: (bi, 0, h)),
        scratch_shapes=[pltpu.VMEM((2 * tq, HEAD), BF16),
                        pltpu.VMEM((1, 2 * tq), F32),
                        pltpu.VMEM((HEAD, 2 * tq), F32)],
        compiler_params=_cparams(("arbitrary", "arbitrary")),
        name="diff_attn_full",
    )(lam, q_src, k_all, vt, norm_w.reshape(1, HEAD))


def _attn_pipe_kernel(lam_ref, q_ref, k_ref, vt_ref, nw_ref, o_ref,
                      qq_ref, sa_ref, sb_ref, pa_ref, pb_ref, ala_ref, alb_ref, ma_ref, mb_ref, m_ref, l_ref,
                      acc_ref, *, tq, tk):
    i = pl.program_id(2)
    _split_maps(q_ref, qq_ref, tq)
    m_ref[...] = jnp.full(m_ref.shape, -jnp.inf, F32)
    l_ref[...] = jnp.zeros(l_ref.shape, F32)
    acc_ref[...] = jnp.zeros(acc_ref.shape, F32)
    s_bufs = (sa_ref, sb_ref)
    p_bufs = (pa_ref, pb_ref)
    al_bufs = (ala_ref, alb_ref)
    mx_bufs = (ma_ref, mb_ref)

    def scores(j, par, masked):
        kb = k_ref[0, pl.ds(pl.multiple_of(j * tk, tk), tk), :]
        s = lax.dot_general(kb, qq_ref[...], (((1,), (1,)), ((), ())), preferred_element_type=F32)
        if masked:
            kchunk = (j * tk + lax.broadcasted_iota(jnp.int32, (tk, 2 * tq), 0)) // CHUNK
            qcol = lax.broadcasted_iota(jnp.int32, (tk, 2 * tq), 1) & (tq - 1)
            s = jnp.where(kchunk <= (i * tq + qcol) // CHUNK, s, -jnp.inf)
        s_bufs[par][...] = s
        mx_bufs[par][...] = jnp.max(s, axis=0, keepdims=True)

    def softmax(par):
        s = s_bufs[par][...]
        m_old = m_ref[...]
        m_new = jnp.maximum(m_old, mx_bufs[par][...])
        alpha = jnp.exp2(m_old - m_new)
        p = jnp.exp2(s - m_new)
        l_ref[...] = alpha * l_ref[...] + jnp.sum(p, axis=0, keepdims=True)
        m_ref[...] = m_new
        p_bufs[par][...] = p.astype(BF16)
        al_bufs[par][...] = alpha

    def pv(j, par):
        upd = jnp.dot(vt_ref[0, 0, j], p_bufs[par][...], preferred_element_type=F32)
        acc_ref[...] = acc_ref[...] * al_bufs[par][...] + upd

    n = 2 * i + 2
    scores(0, 0, True)
    scores(1, 1, True)
    softmax(0)

    def pair(u, carry):
        t = 2 * u + 1
        scores(t + 1, 0, False)
        softmax(1)
        pv(t - 1, 0)
        scores(t + 2, 1, False)
        softmax(0)
        pv(t, 1)
        return carry

    lax.fori_loop(0, i - 1, pair, 0)

    @pl.when(i > 0)
    def _():
        t = n - 3
        scores(t + 1, 0, True)
        softmax(1)
        pv(t - 1, 0)
        scores(t + 2, 1, True)
        softmax(0)
        pv(t, 1)

    softmax(1)
    pv(n - 2, 0)
    pv(n - 1, 1)
    _attn_finish(lam_ref, nw_ref, o_ref, l_ref, acc_ref, tq)


def _diff_attn_prompt(lam, qkv3, vt, norm_w, *, tk):
    b, t = qkv3.shape[0], qkv3.shape[1]
    tq = 2 * tk
    nkb = t // tk
    kern = functools.partial(_attn_pipe_kernel, tq=tq, tk=tk)
    stat = pltpu.VMEM((1, 2 * tq), F32)
    return pl.pallas_call(
        kern,
        out_shape=jax.ShapeDtypeStruct((b, t, GROUP_W), BF16),
        grid=(b, N_HEADS, t // tq),
        in_specs=[pl.BlockSpec(memory_space=pltpu.SMEM),
                  pl.BlockSpec((1, tq, HEAD), lambda bi, h, i: (bi, i, h)),
                  pl.BlockSpec((1, t, HEAD), lambda bi, h, i: (bi, 0, N_HEADS + h)),
                  pl.BlockSpec((1, 1, nkb, HEAD, tk), lambda bi, h, i: (bi, h, 0, 0, 0)),
                  pl.BlockSpec((1, HEAD), lambda bi, h, i: (0, 0))],
        out_specs=pl.BlockSpec((1, tq, HEAD), lambda bi, h, i: (bi, i, h)),
        scratch_shapes=[pltpu.VMEM((2 * tq, HEAD), BF16),
                        pltpu.VMEM((tk, 2 * tq), F32),
                        pltpu.VMEM((tk, 2 * tq), F32),
                        pltpu.VMEM((tk, 2 * tq), BF16),
                        pltpu.VMEM((tk, 2 * tq), BF16),
                        stat, stat, stat, stat, stat, stat,
                        pltpu.VMEM((HEAD, 2 * tq), F32)],
        compiler_params=_cparams(("arbitrary", "arbitrary", "arbitrary")),
        name="diff_attn_prompt",
    )(lam, qkv3, qkv3, vt, norm_w.reshape(1, HEAD))


def _bdot(a, b):
    return jnp.dot(a.astype(BF16), b.astype(BF16), preferred_element_type=F32)


def _bdot_nt(a, b):
    return lax.dot_general(a.astype(BF16), b.astype(BF16), (((1,), (1,)), ((), ())),
                           preferred_element_type=F32)


def _bdot_tn(a, b):
    return lax.dot_general(a.astype(BF16), b.astype(BF16), (((0,), (0,)), ((), ())),
                           preferred_element_type=F32)


def _split3(a):
    hi = a.astype(BF16)
    r = a - hi.astype(F32)
    mid = r.astype(BF16)
    lo = (r - mid.astype(F32)).astype(BF16)
    return hi, mid, lo


def _dot_hp(a, b):
    a_hi = a.astype(BF16)
    a_lo = (a - a_hi.astype(F32)).astype(BF16)
    b_hi = b.astype(BF16)
    b_lo = (b - b_hi.astype(F32)).astype(BF16)
    d = functools.partial(jnp.dot, preferred_element_type=F32)
    return d(a_hi, b_hi) + (d(a_hi, b_lo) + d(a_lo, b_hi))


def _gdn_kernel(qkv_ref, z_ref, ab_ref, convp_ref, s0_ref, cw_ref, alog_ref, dtb_ref, gnw_ref,
                og_ref, sfin_ref, convn_ref, xbuf_ref, s_ref, *, nc):
    c = pl.program_id(1)

    @pl.when(c == 0)
    def _():
        xbuf_ref[0:SUBLANES, :] = convp_ref[0]
        s_ref[...] = s0_ref[0]

    xbuf_ref[SUBLANES:SUBLANES + CHUNK, :] = qkv_ref[0]

    row = lax.broadcasted_iota(jnp.int32, (CHUNK, CHUNK), 0)
    col = lax.broadcasted_iota(jnp.int32, (CHUNK, CHUNK), 1)
    incl = row >= col
    strict = row > col
    blk = (row // 16) == (col // 16)
    eye = (row == col).astype(F32)
    heads = range(N_HEADS)

    ab = ab_ref[0]
    g_all = alog_ref[...] * jax.nn.softplus(ab + dtb_ref[...])
    tri = incl.astype(BF16)
    g_hi, g_mid, g_lo = _split3(g_all)
    d = functools.partial(jnp.dot, preferred_element_type=F32)
    gc_all = d(tri, g_hi) + (d(tri, g_mid) + d(tri, g_lo))
    gc_t = gc_all.T
    beta_all = jax.nn.sigmoid(ab)

    def conv_silu(col0):
        cs = slice(col0, col0 + HEAD)
        y = xbuf_ref[5:5 + CHUNK, cs] * cw_ref[0:1, cs]
        y = y + xbuf_ref[6:6 + CHUNK, cs] * cw_ref[1:2, cs]
        y = y + xbuf_ref[7:7 + CHUNK, cs] * cw_ref[2:3, cs]
        y = y + xbuf_ref[8:8 + CHUNK, cs] * cw_ref[3:4, cs]
        return y * jax.nn.sigmoid(y)

    qr = [conv_silu(h * HEAD) for h in heads]
    kr = [conv_silu(GROUP_W + h * HEAD) for h in heads]
    v = [conv_silu(2 * GROUP_W + h * HEAD) for h in heads]
    qn = [x * lax.rsqrt(jnp.sum(x * x, axis=-1, keepdims=True) + 1e-6) * (HEAD ** -0.5) for x in qr]
    kn = [x * lax.rsqrt(jnp.sum(x * x, axis=-1, keepdims=True) + 1e-6) for x in kr]
    beta = [beta_all[:, SUBLANES + h:SUBLANES + h + 1] for h in heads]
    gc_col = [gc_all[:, h:h + 1] for h in heads]
    g_last = [gc_all[CHUNK - 1:CHUNK, h:h + 1] for h in heads]
    decay = [jnp.where(incl, jnp.exp(jnp.where(incl, gc_col[h] - gc_t[h:h + 1, :], 0.0)), 0.0)
             for h in heads]
    kb = [kn[h] * beta[h] for h in heads]
    a = [jnp.where(strict, _bdot_nt(kb[h], kn[h]) * decay[h], 0.0) for h in heads]
    qk = [jnp.where(incl, _bdot_nt(qn[h], kn[h]) * decay[h], 0.0) for h in heads]
    eg = [jnp.exp(gc_col[h]) for h in heads]
    rhs = [jnp.concatenate([v[h] * beta[h], kb[h] * eg[h]], axis=1) for h in heads]

    ad = [jnp.where(blk, x, 0.0) for x in a]
    ao = [jnp.where(blk, 0.0, x) for x in a]
    a2 = [_bdot(x, x) for x in ad]
    a4 = [_bdot(x, x) for x in a2]
    a8 = [_bdot(x, x) for x in a4]
    p = [eye - x for x in ad]
    p = [p[h] + _bdot(p[h], a2[h]) for h in heads]
    p = [p[h] + _bdot(p[h], a4[h]) for h in heads]
    p = [p[h] + _bdot(p[h], a8[h]) for h in heads]
    y = [_bdot(p[h], rhs[h]) for h in heads]
    bm = [_bdot(p[h], ao[h]) for h in heads]
    b2 = [_bdot(x, x) for x in bm]
    zz = [y[h] + _bdot(b2[h], y[h]) for h in heads]
    sol = [zz[h] - _bdot(bm[h], zz[h]) for h in heads]

    s_old = [s_ref[h] for h in heads]
    v_new = [sol[h][:, 0:HEAD] - _bdot(sol[h][:, HEAD:2 * HEAD], s_old[h]) for h in heads]
    o = [_bdot(qn[h] * eg[h], s_old[h]) + _bdot(qk[h], v_new[h]) for h in heads]
    k_dec = [kn[h] * jnp.exp(g_last[h] - gc_col[h]) for h in heads]
    for h in heads:
        s_ref[h] = s_old[h] * jnp.exp(g_last[h]) + _bdot_tn(k_dec[h], v_new[h])
    for h in heads:
        on = o[h] * lax.rsqrt(jnp.mean(o[h] * o[h], axis=-1, keepdims=True) + EPS) * gnw_ref[...]
        zh = z_ref[0, :, h * HEAD:(h + 1) * HEAD]
        og_ref[0, :, h * HEAD:(h + 1) * HEAD] = (on * (zh * jax.nn.sigmoid(zh))).astype(BF16)

    xbuf_ref[0:SUBLANES, :] = xbuf_ref[CHUNK:CHUNK + SUBLANES, :]

    @pl.when(c == nc - 1)
    def _():
        sfin_ref[0] = s_ref[...]
        convn_ref[0] = xbuf_ref[CHUNK:CHUNK + SUBLANES, :]


def _gdn(proj3, conv_past8, s0, conv_w_t, neg_a, dt_bias, gnw):
    b, t = proj3.shape[0], proj3.shape[1]
    nc = t // CHUNK
    w3 = 3 * GROUP_W
    return pl.pallas_call(
        functools.partial(_gdn_kernel, nc=nc),
        out_shape=(jax.ShapeDtypeStruct((b, t, GROUP_W), BF16),
                   jax.ShapeDtypeStruct((b, N_HEADS, HEAD, HEAD), F32),
                   jax.ShapeDtypeStruct((b, SUBLANES, w3), F32)),
        grid=(b, nc),
        in_specs=[pl.BlockSpec((1, CHUNK, w3), lambda bi, c: (bi, c, COL_QKVG // w3)),
                  pl.BlockSpec((1, CHUNK, GROUP_W), lambda bi, c: (bi, c, COL_Z // GROUP_W)),
                  pl.BlockSpec((1, CHUNK, LANES), lambda bi, c: (bi, c, COL_AB // LANES)),
                  pl.BlockSpec((1, SUBLANES, w3), lambda bi, c: (bi, 0, 0)),
                  pl.BlockSpec((1, N_HEADS, HEAD, HEAD), lambda bi, c: (bi, 0, 0, 0)),
                  pl.BlockSpec((CONV_K, w3), lambda bi, c: (0, 0)),
                  pl.BlockSpec((1, LANES), lambda bi, c: (0, 0)),
                  pl.BlockSpec((1, LANES), lambda bi, c: (0, 0)),
                  pl.BlockSpec((1, HEAD), lambda bi, c: (0, 0))],
        out_specs=(pl.BlockSpec((1, CHUNK, GROUP_W), lambda bi, c: (bi, c, 0)),
                   pl.BlockSpec((1, N_HEADS, HEAD, HEAD), lambda bi, c: (bi, 0, 0, 0)),
                   pl.BlockSpec((1, SUBLANES, w3), lambda bi, c: (bi, 0, 0))),
        scratch_shapes=[pltpu.VMEM((CHUNK + SUBLANES, w3), F32),
                        pltpu.VMEM((N_HEADS, HEAD, HEAD), F32)],
        compiler_params=_cparams(("arbitrary", "arbitrary")),
        name="gdn",
    )(proj3, proj3, proj3, conv_past8, s0, conv_w_t, neg_a, dt_bias, gnw.reshape(1, HEAD))


def _outproj_kernel(od_ref, og_ref, x_ref, w1_ref, w2_ref, fnw_ref, rw_ref, rb_ref,
                    x1_ref, h2_ref, idx_ref, gate_ref):
    acc = jnp.dot(od_ref[...], w1_ref[...], preferred_element_type=F32)
    acc = acc + jnp.dot(og_ref[...], w2_ref[...], preferred_element_type=F32)
    x1 = x_ref[...] + acc
    x1_ref[...] = x1
    h = x1 * lax.rsqrt(jnp.mean(x1 * x1, axis=-1, keepdims=True) + EPS) * fnw_ref[...]
    h2_ref[...] = h
    logits = _dot_hp(h, rw_ref[...]) + rb_ref[...]
    lane = lax.broadcasted_iota(jnp.int32, logits.shape, 1)
    vals = logits
    idx_out = jnp.zeros(logits.shape, jnp.int32)
    top = []
    for k in range(TOP_K):
        mk = jnp.max(vals, axis=-1, keepdims=True)
        ik = jnp.min(jnp.where(vals == mk, lane, LANES), axis=-1, keepdims=True)
        top.append(mk)
        idx_out = jnp.where(lane == k, ik, idx_out)
        vals = jnp.where(lane == ik, -jnp.inf, vals)
    es = [jnp.exp(mk - top[0]) for mk in top]
    inv = 1.0 / (es[0] + es[1] + es[2] + es[3])
    gate_out = jnp.zeros(logits.shape, F32)
    for k in range(TOP_K):
        gate_out = jnp.where(lane == k, es[k] * inv, gate_out)
    idx_ref[...] = idx_out
    gate_ref[...] = gate_out


def _outproj(od, og, x2d, w1, w2, fnw, rw_pad, rb_pad):
    m = x2d.shape[0]
    tm = min(512, m)
    row = lambda i: (i, 0)
    const = lambda i: (0, 0)
    return pl.pallas_call(
        _outproj_kernel,
        out_shape=(jax.ShapeDtypeStruct((m, D_MODEL), F32),
                   jax.ShapeDtypeStruct((m, D_MODEL), F32),
                   jax.ShapeDtypeStruct((m, LANES), jnp.int32),
                   jax.ShapeDtypeStruct((m, LANES), F32)),
        grid=(m // tm,),
        in_specs=[pl.BlockSpec((tm, GROUP_W), row),
                  pl.BlockSpec((tm, GROUP_W), row),
                  pl.BlockSpec((tm, D_MODEL), row),
                  pl.BlockSpec((GROUP_W, D_MODEL), const),
                  pl.BlockSpec((GROUP_W, D_MODEL), const),
                  pl.BlockSpec((1, D_MODEL), const),
                  pl.BlockSpec((D_MODEL, LANES), const),
                  pl.BlockSpec((1, LANES), const)],
        out_specs=(pl.BlockSpec((tm, D_MODEL), row),
                   pl.BlockSpec((tm, D_MODEL), row),
                   pl.BlockSpec((tm, LANES), row),
                   pl.BlockSpec((tm, LANES), row)),
        compiler_params=_cparams(("arbitrary",)),
        name="outproj_router",
    )(od, og, x2d, w1, w2, fnw, rw_pad, rb_pad)


def _cast_kernel(w_ref, o_ref):
    o_ref[...] = w_ref[...].astype(BF16)


def _cast_bf16(w):
    e, r, c = w.shape
    tr = 512
    spec = pl.BlockSpec((1, tr, c), lambda i, j: (i, j, 0))
    return pl.pallas_call(
        _cast_kernel,
        out_shape=jax.ShapeDtypeStruct(w.shape, BF16),
        grid=(e, r // tr),
        in_specs=[spec],
        out_specs=spec,
        compiler_params=_cparams(("arbitrary", "arbitrary")),
        name="cast_bf16",
    )(w)


def _ffn_kernel(be_ref, nu_ref, tok_hbm, h2_hbm, wg_ref, bg_ref, wu_ref, bu_ref, wd_ref, bd_ref, y_ref,
                xa_ref, xb_ref, xbf_ref, ia_ref, ib_ref, isem, rsem, *, nf):
    blk = pl.program_id(0)
    f = pl.program_id(1)
    nu = nu_ref[0]
    slot = blk % 2
    xbufs = (xa_ref, xb_ref)
    idxs = (ia_ref, ib_ref)

    def idx_copy(b, s):
        return pltpu.make_async_copy(tok_hbm.at[b], idxs[s], isem.at[s])

    def issue_rows(s):
        def body(g, carry):
            r0 = pl.multiple_of(g * SUBLANES, SUBLANES)
            for u in range(SUBLANES):
                tok = idxs[s][r0 + u]
                pltpu.make_async_copy(h2_hbm.at[pl.ds(tok, 1)], xbufs[s].at[pl.ds(r0 + u, 1)],
                                      rsem.at[s]).start()
            return carry

        lax.fori_loop(0, EXPERT_BLOCK // SUBLANES, body, 0)

    def wait_rows(s):
        pltpu.make_async_copy(h2_hbm.at[pl.ds(0, EXPERT_BLOCK)], xbufs[s], rsem.at[s]).wait()

    @pl.when(jnp.logical_and(f == 0, blk == 0))
    def _():
        first = idx_copy(0, 0)
        first.start()
        first.wait()
        issue_rows(0)

        @pl.when(1 < nu)
        def _():
            idx_copy(1, 1).start()

    for s in (0, 1):
        @pl.when(jnp.logical_and(jnp.logical_and(f == 0, blk < nu), slot == s))
        def _():
            @pl.when(blk + 1 < nu)
            def _():
                idx_copy(blk + 1, 1 - s).wait()
                issue_rows(1 - s)

            wait_rows(s)
            xbf_ref[...] = xbufs[s][...].astype(BF16)

        @pl.when(jnp.logical_and(jnp.logical_and(f == nf - 1, blk + 2 < nu), slot == s))
        def _():
            idx_copy(blk + 2, s).start()

    @pl.when(blk < nu)
    def _():
        x = xbf_ref[...]
        gate = jnp.minimum(jnp.dot(x, wg_ref[0], preferred_element_type=F32) + bg_ref[0],
                           SWIGLU_LIMIT)
        up = jnp.clip(jnp.dot(x, wu_ref[0], preferred_element_type=F32) + bu_ref[0],
                      -SWIGLU_LIMIT, SWIGLU_LIMIT)
        act = (up + 1.0) * gate * jax.nn.sigmoid(SWIGLU_ALPHA * gate)
        part = jnp.dot(act.astype(BF16), wd_ref[0], preferred_element_type=F32)

        @pl.when(f == 0)
        def _():
            y_ref[...] = part + bd_ref[0]

        @pl.when(f > 0)
        def _():
            y_ref[...] += part

    @pl.when(jnp.logical_and(blk >= nu, f == 0))
    def _():
        y_ref[...] = jnp.zeros(y_ref.shape, F32)


def _moe_ffn(block_e, n_used, slot_tok, h2, wg, bg, wu, bu, wd, bd):
    n_slots = slot_tok.shape[0]
    nb = n_slots // EXPERT_BLOCK
    nf = D_FF // FFN_TF

    def f_of(b, f, nu):
        return jnp.where(b < nu[0], f, nf - 1)

    grid_spec = pltpu.PrefetchScalarGridSpec(
        num_scalar_prefetch=2,
        grid=(nb, nf),
        in_specs=[pl.BlockSpec(memory_space=pl.ANY),
                  pl.BlockSpec(memory_space=pl.ANY),
                  pl.BlockSpec((1, D_MODEL, FFN_TF), lambda b, f, be, nu: (be[b], 0, f_of(b, f, nu))),
                  pl.BlockSpec((1, 1, FFN_TF), lambda b, f, be, nu: (be[b], 0, f_of(b, f, nu))),
                  pl.BlockSpec((1, D_MODEL, FFN_TF), lambda b, f, be, nu: (be[b], 0, f_of(b, f, nu))),
                  pl.BlockSpec((1, 1, FFN_TF), lambda b, f, be, nu: (be[b], 0, f_of(b, f, nu))),
                  pl.BlockSpec((1, FFN_TF, D_MODEL), lambda b, f, be, nu: (be[b], f_of(b, f, nu), 0)),
                  pl.BlockSpec((1, 1, D_MODEL), lambda b, f, be, nu: (be[b], 0, 0))],
        out_specs=pl.BlockSpec((EXPERT_BLOCK, D_MODEL), lambda b, f, be, nu: (b, 0)),
        scratch_shapes=[pltpu.VMEM((EXPERT_BLOCK, D_MODEL), F32),
                        pltpu.VMEM((EXPERT_BLOCK, D_MODEL), F32),
                        pltpu.VMEM((EXPERT_BLOCK, D_MODEL), BF16),
                        pltpu.SMEM((EXPERT_BLOCK,), jnp.int32),
                        pltpu.SMEM((EXPERT_BLOCK,), jnp.int32),
                        pltpu.SemaphoreType.DMA((2,)),
                        pltpu.SemaphoreType.DMA((2,))],
    )
    return pl.pallas_call(
        functools.partial(_ffn_kernel, nf=nf),
        out_shape=jax.ShapeDtypeStruct((n_slots, D_MODEL), F32),
        grid_spec=grid_spec,
        compiler_params=_cparams(("arbitrary", "arbitrary")),
        name="moe_ffn",
    )(block_e, n_used, slot_tok.reshape(nb, EXPERT_BLOCK), h2, wg, bg, wu, bu, wd, bd)


def _combine_kernel(pos_hbm, ys_hbm, x1_ref, gate_ref, fnw_ref, y_ref, pa_ref, pb_ref, ya_ref, yb_ref,
                    isem, rsem, *, tm, nb):
    blk = pl.program_id(0)
    slot = blk % 2
    poss = (pa_ref, pb_ref)
    ybufs = (ya_ref, yb_ref)

    def idx_copy(b, s):
        return pltpu.make_async_copy(pos_hbm.at[b], poss[s], isem.at[s])

    def issue_rows(s):
        def body(g, carry):
            t0 = pl.multiple_of(g * SUBLANES, SUBLANES)
            for u in range(SUBLANES):
                for k in range(TOP_K):
                    p = poss[s][(t0 + u) * TOP_K + k]
                    pltpu.make_async_copy(ys_hbm.at[pl.ds(p, 1)], ybufs[s].at[k, pl.ds(t0 + u, 1)],
                                          rsem.at[s]).start()
            return carry

        lax.fori_loop(0, tm // SUBLANES, body, 0)

    def wait_rows(s):
        for k in range(TOP_K):
            pltpu.make_async_copy(ys_hbm.at[pl.ds(0, tm)], ybufs[s].at[k], rsem.at[s]).wait()

    @pl.when(blk == 0)
    def _():
        first = idx_copy(0, 0)
        first.start()
        first.wait()
        issue_rows(0)
        if nb > 1:
            idx_copy(1, 1).start()

    for s in (0, 1):
        @pl.when(jnp.logical_and(blk + 1 < nb, slot == s))
        def _():
            idx_copy(blk + 1, 1 - s).wait()
            issue_rows(1 - s)

        @pl.when(jnp.logical_and(blk + 2 < nb, slot == s))
        def _():
            idx_copy(blk + 2, s).start()

        @pl.when(slot == s)
        def _():
            wait_rows(s)
            g = gate_ref[...]
            moe = g[:, 0:1] * ybufs[s][0]
            for k in range(1, TOP_K):
                moe = moe + g[:, k:k + 1] * ybufs[s][k]
            x2 = x1_ref[...] + moe
            y_ref[...] = (x2 * lax.rsqrt(jnp.mean(x2 * x2, axis=-1, keepdims=True) + EPS)
                          * fnw_ref[...])


def _combine(pos, y_slots, x1, gates, final_norm_w):
    m = x1.shape[0]
    tm = min(COMBINE_TM, m)
    nb = m // tm
    row = lambda i: (i, 0)
    return pl.pallas_call(
        functools.partial(_combine_kernel, tm=tm, nb=nb),
        out_shape=jax.ShapeDtypeStruct((m, D_MODEL), F32),
        grid=(nb,),
        in_specs=[pl.BlockSpec(memory_space=pl.ANY),
                  pl.BlockSpec(memory_space=pl.ANY),
                  pl.BlockSpec((tm, D_MODEL), row),
                  pl.BlockSpec((tm, LANES), row),
                  pl.BlockSpec((1, D_MODEL), lambda i: (0, 0))],
        out_specs=pl.BlockSpec((tm, D_MODEL), row),
        scratch_shapes=[pltpu.SMEM((tm * TOP_K,), jnp.int32),
                        pltpu.SMEM((tm * TOP_K,), jnp.int32),
                        pltpu.VMEM((TOP_K, tm, D_MODEL), F32),
                        pltpu.VMEM((TOP_K, tm, D_MODEL), F32),
                        pltpu.SemaphoreType.DMA((2,)),
                        pltpu.SemaphoreType.DMA((2,))],
        compiler_params=_cparams(("arbitrary",)),
        name="moe_combine",
    )(pos.reshape(nb, tm * TOP_K), y_slots, x1, gates, final_norm_w.reshape(1, D_MODEL))


def _route(top_idx):
    n_tok = top_idx.shape[0]
    n_assign = n_tok * TOP_K
    flat_e = top_idx.reshape(n_assign)
    onehot = (flat_e[:, None] == jnp.arange(N_EXPERTS, dtype=jnp.int32)[None, :]).astype(jnp.int32)
    csum = jnp.cumsum(onehot, axis=0)
    rank = jnp.sum(csum * onehot, axis=1) - 1
    counts = csum[-1]
    padded = (counts + EXPERT_BLOCK - 1) // EXPERT_BLOCK * EXPERT_BLOCK
    padded_end = jnp.cumsum(padded)
    padded_start = padded_end - padded
    dest = (jnp.sum(onehot * padded_start[None, :], axis=1) + rank).astype(jnp.int32)
    n_blocks = -(-n_assign // EXPERT_BLOCK) + N_EXPERTS
    flat_tok = jnp.arange(n_assign, dtype=jnp.int32) // TOP_K
    slot_tok = jnp.zeros((n_blocks * EXPERT_BLOCK,), jnp.int32).at[dest].set(flat_tok)
    starts = jnp.arange(n_blocks, dtype=jnp.int32) * EXPERT_BLOCK
    block_e = jnp.minimum(jnp.sum((padded_end[None, :] <= starts[:, None]).astype(jnp.int32), axis=1),
                          N_EXPERTS - 1).astype(jnp.int32)
    n_used = (padded_end[-1] // EXPERT_BLOCK).astype(jnp.int32).reshape(1)
    return slot_tok, dest, block_e, n_used


def _mixer(x, pos, k_cache, v_cache, conv_past, s0, lam, attn_norm_w, w_perm, conv_w_t, neg_a, dt_b,
           diff_norm_w, gdn_norm_w):
    b, t, _ = x.shape
    m = b * t
    tm = min(1024, m)
    tabs = _rope_tables(pos)
    if t < tm:
        tabs = tuple(jnp.tile(a, (tm // t, 1)) for a in tabs)
    proj, qkv = _inproj(x.reshape(m, D_MODEL), attn_norm_w, w_perm, tabs)
    proj3 = proj.reshape(b, t, PROJ_COLS)
    qkv3 = qkv.reshape(b, t, 3 * GROUP_W)
    k_rows = proj3[:, :, GROUP_W:2 * GROUP_W].reshape(b, t, N_HEADS, HEAD)
    v_rows = proj3[:, :, 2 * GROUP_W:3 * GROUP_W].reshape(b, t, N_HEADS, HEAD)

    if k_cache is None:
        tk = min(512, t // 4)
        vt = qkv3[:, :, 2 * GROUP_W:].reshape(b, t // tk, tk, N_HEADS, HEAD).transpose(0, 3, 1, 4, 2)
        o_d = _diff_attn_prompt(lam, qkv3, vt, diff_norm_w, tk=tk)
    else:
        past = k_cache.shape[1]
        k_all = jnp.concatenate([k_cache.reshape(b, past, GROUP_W).astype(BF16),
                                 qkv3[:, :, GROUP_W:2 * GROUP_W]], axis=1)
        v_all = jnp.concatenate([v_cache.reshape(b, past, N_HEADS, HEAD).astype(BF16),
                                 qkv3[:, :, 2 * GROUP_W:].reshape(b, t, N_HEADS, HEAD)], axis=1)
        vt = v_all.transpose(0, 2, 3, 1)[:, :, None]
        o_d = _diff_attn_full(lam, qkv3, k_all, vt, diff_norm_w)

    conv_past8 = jnp.pad(conv_past, ((0, 0), (SUBLANES - (CONV_K - 1), 0), (0, 0)))
    o_g, s_new, conv8 = _gdn(proj3, conv_past8, s0, conv_w_t, neg_a, dt_b, gdn_norm_w)
    conv_new = conv8[:, SUBLANES - (CONV_K - 1):, :]
    return o_d.reshape(m, GROUP_W), o_g.reshape(m, GROUP_W), k_rows, v_rows, conv_new, s_new


def kernel(x_prompt, x_sample, cache_k_diff, cache_v_diff, state_conv, state_gdn, attn_norm_w, w_in,
           conv_w, a_log, dt_bias, lambda_q1, lambda_k1, lambda_q2, lambda_k2, diff_norm_w, gdn_norm_w,
           w_out, ffn_norm_w, router_w, router_b, w_gate, b_gate, w_up, b_up, w_down, b_down,
           final_norm_w):
    l = 0
    bp, tp, _ = x_prompt.shape
    bs, ts, _ = x_sample.shape
    past_len = cache_k_diff.shape[2]

    lam = (jnp.exp(jnp.sum(lambda_q1[l] * lambda_k1[l])) - jnp.exp(jnp.sum(lambda_q2[l] * lambda_k2[l]))
           + LAM_INIT).astype(F32).reshape(1, 1)

    wl = w_in[l]
    ncol = wl.shape[1]
    w_perm = jnp.concatenate(
        [wl[:, :COL_Z], wl[:, ncol - GROUP_W:], wl[:, COL_Z:COL_Z + 2 * N_HEADS],
         jnp.zeros((D_MODEL, PROJ_COLS - ncol), F32)], axis=1).astype(BF16)
    conv_w_t = conv_w[l].T
    pad8 = lambda a: jnp.pad(a.astype(F32), (0, LANES - N_HEADS)).reshape(1, LANES)
    neg_a = pad8(-jnp.exp(a_log[l]))
    dt_b = pad8(dt_bias[l])
    w1 = w_out[l][:GROUP_W].astype(BF16)
    w2 = w_out[l][GROUP_W:].astype(BF16)
    rw_pad = jnp.pad(router_w[l], ((0, 0), (0, LANES - N_EXPERTS)))
    rb_pad = jnp.concatenate([router_b[l].astype(F32),
                              jnp.full((LANES - N_EXPERTS,), -jnp.inf, F32)]).reshape(1, LANES)
    wg = _cast_bf16(w_gate[l])
    wu = _cast_bf16(w_up[l])
    wd = _cast_bf16(w_down[l])
    bg = b_gate[l].reshape(N_EXPERTS, 1, D_FF)
    bu = b_up[l].reshape(N_EXPERTS, 1, D_FF)
    bd = b_down[l].reshape(N_EXPERTS, 1, D_MODEL)

    mix_args = (lam, attn_norm_w[l], w_perm, conv_w_t, neg_a, dt_b, diff_norm_w[l], gdn_norm_w[l])
    conv0 = jnp.zeros((bp, CONV_K - 1, 3 * GROUP_W), F32)
    s0 = jnp.zeros((bp, N_HEADS, HEAD, HEAD), F32)
    od_p, og_p, kp, vp, cp, sp = _mixer(x_prompt, jnp.arange(tp), None, None, conv0, s0, *mix_args)
    od_s, og_s, ks, vs, cs, ss = _mixer(x_sample, past_len + jnp.arange(ts), cache_k_diff[l],
                                        cache_v_diff[l], state_conv[l], state_gdn[l], *mix_args)

    fnw = ffn_norm_w[l].reshape(1, D_MODEL)
    x1_p, h2_p, idx_p, gate_p = _outproj(od_p, og_p, x_prompt.reshape(bp * tp, D_MODEL), w1, w2, fnw,
                                         rw_pad, rb_pad)
    x1_s, h2_s, idx_s, gate_s = _outproj(od_s, og_s, x_sample.reshape(bs * ts, D_MODEL), w1, w2, fnw,
                                         rw_pad, rb_pad)

    n_p = bp * tp
    h2 = jnp.concatenate([h2_p, h2_s], axis=0)
    top_idx = jnp.concatenate([idx_p[:, :TOP_K], idx_s[:, :TOP_K]], axis=0)
    slot_tok, dest, block_e, n_used = _route(top_idx)
    y_slots = _moe_ffn(block_e, n_used, slot_tok, h2, wg, bg, wu, bu, wd, bd)
    y_p = _combine(dest[:n_p * TOP_K], y_slots, x1_p, gate_p, final_norm_w)
    y_s = _combine(dest[n_p * TOP_K:], y_slots, x1_s, gate_s, final_norm_w)

    return (y_p.reshape(bp, tp, D_MODEL), y_s.reshape(bs, ts, D_MODEL),
            kp[None], vp[None], cp[None], sp[None], ks[None], vs[None], cs[None], ss[None])
```

```python
import functools
import math

import jax
import jax.numpy as jnp
from jax import lax
from jax.experimental import pallas as pl
from jax.experimental.pallas import tpu as pltpu

F32 = jnp.float32
BF16 = jnp.bfloat16

D_MODEL = 2048
N_HEADS = 8
HEAD = 128
QK_DIM = 64
ROT_DIM = 16
ROT_HALF = ROT_DIM // 2
ROPE_THETA = 500000.0
CHUNK = 64
GROUP_W = N_HEADS * HEAD
CONV_K = 4
N_EXPERTS = 32
TOP_K = 4
D_FF = 2048
SWIGLU_LIMIT = 7.0
SWIGLU_ALPHA = 1.702
EPS = 1e-6
SUBLN_EPS = 1e-5
LAM_INIT = 0.8 - 0.6 * math.exp(-0.3 * 0)
Q_SCALE = QK_DIM ** -0.5 * math.log2(math.e)

LANES = 128
SUBLANES = 8
VMEM_LIMIT = 56 * 1024 * 1024

PROJ_TN = 512
PROJ_COLS = 7680
COL_QKVG = 3 * GROUP_W
COL_Z = 6 * GROUP_W
COL_AB = 7 * GROUP_W

EXPERT_BLOCK = 512
FFN_TF = 1024
COMBINE_TM = 256
GDN_BATCH = 2


def _cparams(sem, vmem=VMEM_LIMIT):
    return pltpu.CompilerParams(dimension_semantics=sem, vmem_limit_bytes=vmem)


def _inproj_kernel(x_ref, nw_ref, w_ref, c_ref, s1_ref, s2_ref, proj_ref, qkv_ref, h_ref):
    j = pl.program_id(1)

    @pl.when(j == 0)
    def _():
        x = x_ref[...]
        ms = jnp.mean(x * x, axis=-1, keepdims=True)
        h_ref[...] = (x * lax.rsqrt(ms + EPS) * nw_ref[...]).astype(BF16)

    acc = jnp.dot(h_ref[...], w_ref[...], preferred_element_type=F32)

    @pl.when(j < 4)
    def _():
        c = c_ref[...]
        s1 = s1_ref[...]
        s2 = s2_ref[...]
        scale = jnp.where(j < 2, Q_SCALE, 1.0).astype(F32)
        for g in range(PROJ_TN // LANES):
            xg = acc[:, g * LANES:(g + 1) * LANES]
            r = (xg * c + pltpu.roll(xg, LANES - ROT_HALF, 1) * s1
                 + pltpu.roll(xg, ROT_HALF, 1) * s2)
            proj_ref[:, g * LANES:(g + 1) * LANES] = r
            qkv_ref[:, g * LANES:(g + 1) * LANES] = (r * scale).astype(BF16)

    @pl.when(jnp.logical_and(j >= 4, j < 6))
    def _():
        proj_ref[...] = acc
        qkv_ref[...] = acc.astype(BF16)

    @pl.when(j >= 6)
    def _():
        proj_ref[...] = acc


def _rope_tables(pos):
    inv_freq = jnp.power(ROPE_THETA, -jnp.arange(ROT_HALF, dtype=F32) * (2.0 / ROT_DIM))
    ang = pos.astype(F32)[:, None] * inv_freq[None, :]
    cos, sin = jnp.cos(ang), jnp.sin(ang)
    t = pos.shape[0]
    one = jnp.ones((t, QK_DIM - ROT_DIM), F32)
    zero = jnp.zeros((t, QK_DIM - ROT_DIM), F32)
    z8 = jnp.zeros((t, ROT_HALF), F32)
    c = jnp.concatenate([cos, cos, one], axis=1)
    s1 = jnp.concatenate([-sin, z8, zero], axis=1)
    s2 = jnp.concatenate([z8, sin, zero], axis=1)
    return tuple(jnp.concatenate([a, a], axis=1) for a in (c, s1, s2))


def _inproj(x2d, norm_w, w_perm, tables):
    m = x2d.shape[0]
    tm = min(1024, m)
    tab_rows = tables[0].shape[0]
    n_tab = tab_rows // tm
    grid = (m // tm, PROJ_COLS // PROJ_TN)
    tab_spec = pl.BlockSpec((tm, LANES), lambda i, j: (i % n_tab, 0))
    return pl.pallas_call(
        _inproj_kernel,
        out_shape=(jax.ShapeDtypeStruct((m, PROJ_COLS), F32),
                   jax.ShapeDtypeStruct((m, 3 * GROUP_W), BF16)),
        grid=grid,
        in_specs=[pl.BlockSpec((tm, D_MODEL), lambda i, j: (i, 0)),
                  pl.BlockSpec((1, D_MODEL), lambda i, j: (0, 0)),
                  pl.BlockSpec((D_MODEL, PROJ_TN), lambda i, j: (0, j)),
                  tab_spec, tab_spec, tab_spec],
        out_specs=(pl.BlockSpec((tm, PROJ_TN), lambda i, j: (i, j)),
                   pl.BlockSpec((tm, PROJ_TN), lambda i, j: (i, jnp.minimum(j, 5)))),
        scratch_shapes=[pltpu.VMEM((tm, D_MODEL), BF16)],
        compiler_params=_cparams(("arbitrary", "arbitrary")),
        name="inproj",
    )(x2d, norm_w.reshape(1, D_MODEL), w_perm, *tables)


def _split_maps(q_ref, qq_ref, tq):
    q = q_ref[0]
    lane = lax.broadcasted_iota(jnp.int32, (tq, LANES), 1)
    zero = jnp.zeros_like(q)
    qq_ref[0:tq, :] = jnp.where(lane < QK_DIM, q, zero)
    qq_ref[tq:2 * tq, :] = jnp.where(lane >= QK_DIM, q, zero)


def _attn_finish(lam_ref, nw_ref, o_ref, l_ref, acc_ref, tq):
    lam = lam_ref[0, 0]
    inv_l = 1.0 / l_ref[...]
    o = acc_ref[...] * inv_l
    o = o[:, 0:tq] - lam * o[:, tq:2 * tq]
    ms = jnp.mean(o * o, axis=0, keepdims=True)
    on = (o * lax.rsqrt(ms + SUBLN_EPS)).T
    o_ref[0] = ((on * nw_ref[...]) * (1.0 - LAM_INIT)).astype(BF16)


def _attn_full_kernel(lam_ref, q_ref, k_ref, vt_ref, nw_ref, o_ref, qq_ref, l_ref, acc_ref, *, tq):
    _split_maps(q_ref, qq_ref, tq)
    s = lax.dot_general(k_ref[0], qq_ref[...], (((1,), (1,)), ((), ())),
                        preferred_element_type=F32)
    p = jnp.exp2(s - jnp.max(s, axis=0, keepdims=True))
    l_ref[...] = jnp.sum(p, axis=0, keepdims=True)
    acc_ref[...] = jnp.dot(vt_ref[0, 0, 0], p.astype(BF16), preferred_element_type=F32)
    _attn_finish(lam_ref, nw_ref, o_ref, l_ref, acc_ref, tq)


def _diff_attn_full(lam, q_src, k_all, vt, norm_w):
    b, tq = q_src.shape[0], q_src.shape[1]
    t_k = k_all.shape[1]
    return pl.pallas_call(
        functools.partial(_attn_full_kernel, tq=tq),
        out_shape=jax.ShapeDtypeStruct((b, tq, GROUP_W), BF16),
        grid=(b, N_HEADS),
        in_specs=[pl.BlockSpec(memory_space=pltpu.SMEM),
                  pl.BlockSpec((1, tq, HEAD), lambda bi, h: (bi, 0, h)),
                  pl.BlockSpec((1, t_k, HEAD), lambda bi, h: (bi, 0, h)),
                  pl.BlockSpec((1, 1, 1, HEAD, t_k), lambda bi, h: (bi, h, 0, 0, 0)),
                  pl.BlockSpec((1, HEAD), lambda bi, h: (0, 0))],
        out_specs=pl.BlockSpec((1, tq, HEAD), lambda bi, h: (bi, 0, h)),
        scratch_shapes=[pltpu.VMEM((2 * tq, HEAD), BF16),
                        pltpu.VMEM((1, 2 * tq), F32),
                        pltpu.VMEM((HEAD, 2 * tq), F32)],
        compiler_params=_cparams(("arbitrary", "arbitrary")),
        name="diff_attn_full",
    )(lam, q_src, k_all, vt, norm_w.reshape(1, HEAD))


def _attn_pipe_kernel(lam_ref, q_ref, k_ref, vt_ref, nw_ref, o_ref,
                      qq_ref, sa_ref, sb_ref, pa_ref, pb_ref, ala_ref, alb_ref, ma_ref, mb_ref, m_ref, l_ref,
                      acc_ref, *, tq, tk):
    i = pl.program_id(2)
    _split_maps(q_ref, qq_ref, tq)
    m_ref[...] = jnp.full(m_ref.shape, -jnp.inf, F32)
    l_ref[...] = jnp.zeros(l_ref.shape, F32)
    acc_ref[...] = jnp.zeros(acc_ref.shape, F32)
    s_bufs = (sa_ref, sb_ref)
    p_bufs = (pa_ref, pb_ref)
    al_bufs = (ala_ref, alb_ref)
    mx_bufs = (ma_ref, mb_ref)

    def scores(j, par, masked):
        kb = k_ref[0, pl.ds(pl.multiple_of(j * tk, tk), tk), :]
        s = lax.dot_general(kb, qq_ref[...], (((1,), (1,)), ((), ())), preferred_element_type=F32)
        if masked:
            kchunk = (j * tk + lax.broadcasted_iota(jnp.int32, (tk, 2 * tq), 0)) // CHUNK
            qcol = lax.broadcasted_iota(jnp.int32, (tk, 2 * tq), 1) & (tq - 1)
            s = jnp.where(kchunk <= (i * tq + qcol) // CHUNK, s, -jnp.inf)
        s_bufs[par][...] = s
        mx_bufs[par][...] = jnp.max(s, axis=0, keepdims=True)

    def softmax(par):
        s = s_bufs[par][...]
        m_old = m_ref[...]
        m_new = jnp.maximum(m_old, mx_bufs[par][...])
        alpha = jnp.exp2(m_old - m_new)
        p = jnp.exp2(s - m_new)
        l_ref[...] = alpha * l_ref[...] + jnp.sum(p, axis=0, keepdims=True)
        m_ref[...] = m_new
        p_bufs[par][...] = p.astype(BF16)
        al_bufs[par][...] = alpha

    def pv(j, par):
        upd = jnp.dot(vt_ref[0, 0, j], p_bufs[par][...], preferred_element_type=F32)
        acc_ref[...] = acc_ref[...] * al_bufs[par][...] + upd

    n = 2 * i + 2
    scores(0, 0, True)
    scores(1, 1, True)
    softmax(0)

    def pair(u, carry):
        t = 2 * u + 1
        scores(t + 1, 0, False)
        softmax(1)
        pv(t - 1, 0)
        scores(t + 2, 1, False)
        softmax(0)
        pv(t, 1)
        return carry

    lax.fori_loop(0, i - 1, pair, 0)

    @pl.when(i > 0)
    def _():
        t = n - 3
        scores(t + 1, 0, True)
        softmax(1)
        pv(t - 1, 0)
        scores(t + 2, 1, True)
        softmax(0)
        pv(t, 1)

    softmax(1)
    pv(n - 2, 0)
    pv(n - 1, 1)
    _attn_finish(lam_ref, nw_ref, o_ref, l_ref, acc_ref, tq)


def _diff_attn_prompt(lam, qkv3, vt, norm_w, *, tk):
    b, t = qkv3.shape[0], qkv3.shape[1]
    tq = 2 * tk
    nkb = t // tk
    kern = functools.partial(_attn_pipe_kernel, tq=tq, tk=tk)
    stat = pltpu.VMEM((1, 2 * tq), F32)
    return pl.pallas_call(
        kern,
        out_shape=jax.ShapeDtypeStruct((b, t, GROUP_W), BF16),
        grid=(b, N_HEADS, t // tq),
        in_specs=[pl.BlockSpec(memory_space=pltpu.SMEM),
                  pl.BlockSpec((1, tq, HEAD), lambda bi, h, i: (bi, i, h)),
                  pl.BlockSpec((1, t, HEAD), lambda bi, h, i: (bi, 0, N_HEADS + h)),
                  pl.BlockSpec((1, 1, nkb, HEAD, tk), lambda bi, h, i: (bi, h, 0, 0, 0)),
                  pl.BlockSpec((1, HEAD), lambda bi, h, i: (0, 0))],
        out_specs=pl.BlockSpec((1, tq, HEAD), lambda bi, h, i: (bi, i, h)),
        scratch_shapes=[pltpu.VMEM((2 * tq, HEAD), BF16),
                        pltpu.VMEM((tk, 2 * tq), F32),
                        pltpu.VMEM((tk, 2 * tq), F32),
                        pltpu.VMEM((tk, 2 * tq), BF16),
                        pltpu.VMEM((tk, 2 * tq), BF16),
                        stat, stat, stat, stat, stat, stat,
                        pltpu.VMEM((HEAD, 2 * tq), F32)],
        compiler_params=_cparams(("arbitrary", "arbitrary", "arbitrary")),
        name="diff_attn_prompt",
    )(lam, qkv3, qkv3, vt, norm_w.reshape(1, HEAD))


def _bdot(a, b):
    return jnp.dot(a.astype(BF16), b.astype(BF16), preferred_element_type=F32)


def _bdot_nt(a, b):
    return lax.dot_general(a.astype(BF16), b.astype(BF16), (((1,), (1,)), ((), ())),
                           preferred_element_type=F32)


def _bdot_tn(a, b):
    return lax.dot_general(a.astype(BF16), b.astype(BF16), (((0,), (0,)), ((), ())),
                           preferred_element_type=F32)


def _split3(a):
    hi = a.astype(BF16)
    r = a - hi.astype(F32)
    mid = r.astype(BF16)
    lo = (r - mid.astype(F32)).astype(BF16)
    return hi, mid, lo


def _dot_hp(a, b):
    a_hi = a.astype(BF16)
    a_lo = (a - a_hi.astype(F32)).astype(BF16)
    b_hi = b.astype(BF16)
    b_lo = (b - b_hi.astype(F32)).astype(BF16)
    d = functools.partial(jnp.dot, preferred_element_type=F32)
    return d(a_hi, b_hi) + (d(a_hi, b_lo) + d(a_lo, b_hi))


def _gdn_kernel(qkv_ref, z_ref, ab_ref, convp_ref, s0_ref, cw_ref, alog_ref, dtb_ref, gnw_ref,
                og_ref, sfin_ref, convn_ref, xbuf_ref, s_ref, *, nc, nb):
    c = pl.program_id(1)

    @pl.when(c == 0)
    def _():
        xbuf_ref[:, 0:SUBLANES, :] = convp_ref[...]
        s_ref[...] = s0_ref[...]

    xbuf_ref[:, SUBLANES:SUBLANES + CHUNK, :] = qkv_ref[...]

    row = lax.broadcasted_iota(jnp.int32, (CHUNK, CHUNK), 0)
    col = lax.broadcasted_iota(jnp.int32, (CHUNK, CHUNK), 1)
    incl = row >= col
    strict = row > col
    blk = (row // 16) == (col // 16)
    eye = (row == col).astype(F32)
    tri = incl.astype(BF16)
    d = functools.partial(jnp.dot, preferred_element_type=F32)
    S = [(bb, h) for bb in range(nb) for h in range(N_HEADS)]

    gc_all, gc_t, beta_all = {}, {}, {}
    for bb in range(nb):
        ab = ab_ref[bb]
        g_all = alog_ref[...] * jax.nn.softplus(ab + dtb_ref[...])
        g_hi, g_mid, g_lo = _split3(g_all)
        gc_all[bb] = d(tri, g_hi) + (d(tri, g_mid) + d(tri, g_lo))
        gc_t[bb] = gc_all[bb].T
        beta_all[bb] = jax.nn.sigmoid(ab)

    def conv_silu(bb, col0):
        cs = slice(col0, col0 + HEAD)
        y = xbuf_ref[bb, 5:5 + CHUNK, cs] * cw_ref[0:1, cs]
        y = y + xbuf_ref[bb, 6:6 + CHUNK, cs] * cw_ref[1:2, cs]
        y = y + xbuf_ref[bb, 7:7 + CHUNK, cs] * cw_ref[2:3, cs]
        y = y + xbuf_ref[bb, 8:8 + CHUNK, cs] * cw_ref[3:4, cs]
        return y * jax.nn.sigmoid(y)

    qr = {k: conv_silu(k[0], k[1] * HEAD) for k in S}
    kr = {k: conv_silu(k[0], GROUP_W + k[1] * HEAD) for k in S}
    v = {k: conv_silu(k[0], 2 * GROUP_W + k[1] * HEAD) for k in S}
    qn = {k: qr[k] * lax.rsqrt(jnp.sum(qr[k] * qr[k], axis=-1, keepdims=True) + 1e-6) * (HEAD ** -0.5)
          for k in S}
    kn = {k: kr[k] * lax.rsqrt(jnp.sum(kr[k] * kr[k], axis=-1, keepdims=True) + 1e-6) for k in S}
    beta = {k: beta_all[k[0]][:, SUBLANES + k[1]:SUBLANES + k[1] + 1] for k in S}
    gc_col = {k: gc_all[k[0]][:, k[1]:k[1] + 1] for k in S}
    g_last = {k: gc_all[k[0]][CHUNK - 1:CHUNK, k[1]:k[1] + 1] for k in S}
    decay = {k: jnp.where(incl, jnp.exp(jnp.where(incl, gc_col[k] - gc_t[k[0]][k[1]:k[1] + 1, :], 0.0)), 0.0)
             for k in S}
    kb = {k: kn[k] * beta[k] for k in S}
    a = {k: jnp.where(strict, _bdot_nt(kb[k], kn[k]) * decay[k], 0.0) for k in S}
    qk = {k: jnp.where(incl, _bdot_nt(qn[k], kn[k]) * decay[k], 0.0) for k in S}
    eg = {k: jnp.exp(gc_col[k]) for k in S}
    rhs = {k: jnp.concatenate([v[k] * beta[k], kb[k] * eg[k]], axis=1) for k in S}

    ad = {k: jnp.where(blk, a[k], 0.0) for k in S}
    ao = {k: jnp.where(blk, 0.0, a[k]) for k in S}
    a2 = {k: _bdot(ad[k], ad[k]) for k in S}
    a4 = {k: _bdot(a2[k], a2[k]) for k in S}
    a8 = {k: _bdot(a4[k], a4[k]) for k in S}
    p = {k: eye - ad[k] for k in S}
    p = {k: p[k] + _bdot(p[k], a2[k]) for k in S}
    p = {k: p[k] + _bdot(p[k], a4[k]) for k in S}
    p = {k: p[k] + _bdot(p[k], a8[k]) for k in S}
    y = {k: _bdot(p[k], rhs[k]) for k in S}
    bm = {k: _bdot(p[k], ao[k]) for k in S}
    b2 = {k: _bdot(bm[k], bm[k]) for k in S}
    zz = {k: y[k] + _bdot(b2[k], y[k]) for k in S}
    sol = {k: zz[k] - _bdot(bm[k], zz[k]) for k in S}

    s_old = {k: s_ref[k[0], k[1]] for k in S}
    v_new = {k: sol[k][:, 0:HEAD] - _bdot(sol[k][:, HEAD:2 * HEAD], s_old[k]) for k in S}
    o = {k: _bdot(qn[k] * eg[k], s_old[k]) + _bdot(qk[k], v_new[k]) for k in S}
    k_dec = {k: kn[k] * jnp.exp(g_last[k] - gc_col[k]) for k in S}
    for k in S:
        s_ref[k[0], k[1]] = s_old[k] * jnp.exp(g_last[k]) + _bdot_tn(k_dec[k], v_new[k])
    for k in S:
        bb, h = k
        on = o[k] * lax.rsqrt(jnp.mean(o[k] * o[k], axis=-1, keepdims=True) + EPS) * gnw_ref[...]
        zh = z_ref[bb, :, h * HEAD:(h + 1) * HEAD]
        og_ref[bb, :, h * HEAD:(h + 1) * HEAD] = (on * (zh * jax.nn.sigmoid(zh))).astype(BF16)

    xbuf_ref[:, 0:SUBLANES, :] = xbuf_ref[:, CHUNK:CHUNK + SUBLANES, :]

    @pl.when(c == nc - 1)
    def _():
        sfin_ref[...] = s_ref[...]
        convn_ref[...] = xbuf_ref[:, CHUNK:CHUNK + SUBLANES, :]


def _gdn(proj3, conv_past8, s0, conv_w_t, neg_a, dt_bias, gnw):
    b, t = proj3.shape[0], proj3.shape[1]
    nc = t // CHUNK
    nb = GDN_BATCH if b % GDN_BATCH == 0 else 1
    w3 = 3 * GROUP_W
    return pl.pallas_call(
        functools.partial(_gdn_kernel, nc=nc, nb=nb),
        out_shape=(jax.ShapeDtypeStruct((b, t, GROUP_W), BF16),
                   jax.ShapeDtypeStruct((b, N_HEADS, HEAD, HEAD), F32),
                   jax.ShapeDtypeStruct((b, SUBLANES, w3), F32)),
        grid=(b // nb, nc),
        in_specs=[pl.BlockSpec((nb, CHUNK, w3), lambda bi, c: (bi, c, COL_QKVG // w3)),
                  pl.BlockSpec((nb, CHUNK, GROUP_W), lambda bi, c: (bi, c, COL_Z // GROUP_W)),
                  pl.BlockSpec((nb, CHUNK, LANES), lambda bi, c: (bi, c, COL_AB // LANES)),
                  pl.BlockSpec((nb, SUBLANES, w3), lambda bi, c: (bi, 0, 0)),
                  pl.BlockSpec((nb, N_HEADS, HEAD, HEAD), lambda bi, c: (bi, 0, 0, 0)),
                  pl.BlockSpec((CONV_K, w3), lambda bi, c: (0, 0)),
                  pl.BlockSpec((1, LANES), lambda bi, c: (0, 0)),
                  pl.BlockSpec((1, LANES), lambda bi, c: (0, 0)),
                  pl.BlockSpec((1, HEAD), lambda bi, c: (0, 0))],
        out_specs=(pl.BlockSpec((nb, CHUNK, GROUP_W), lambda bi, c: (bi, c, 0)),
                   pl.BlockSpec((nb, N_HEADS, HEAD, HEAD), lambda bi, c: (bi, 0, 0, 0)),
                   pl.BlockSpec((nb, SUBLANES, w3), lambda bi, c: (bi, 0, 0))),
        scratch_shapes=[pltpu.VMEM((nb, CHUNK + SUBLANES, w3), F32),
                        pltpu.VMEM((nb, N_HEADS, HEAD, HEAD), F32)],
        compiler_params=_cparams(("arbitrary", "arbitrary")),
        name="gdn",
    )(proj3, proj3, proj3, conv_past8, s0, conv_w_t, neg_a, dt_bias, gnw.reshape(1, HEAD))


def _outproj_kernel(od_ref, og_ref, x_ref, w1_ref, w2_ref, fnw_ref, rw_ref, rb_ref,
                    x1_ref, h2_ref, idx_ref, gate_ref, rank_ref, cnt_ref):
    @pl.when(pl.program_id(0) == 0)
    def _():
        cnt_ref[...] = jnp.zeros(cnt_ref.shape, F32)

    acc = jnp.dot(od_ref[...], w1_ref[...], preferred_element_type=F32)
    acc = acc + jnp.dot(og_ref[...], w2_ref[...], preferred_element_type=F32)
    x1 = x_ref[...] + acc
    x1_ref[...] = x1
    h = x1 * lax.rsqrt(jnp.mean(x1 * x1, axis=-1, keepdims=True) + EPS) * fnw_ref[...]
    h2_ref[...] = h
    logits = _dot_hp(h, rw_ref[...]) + rb_ref[...]
    lane = lax.broadcasted_iota(jnp.int32, logits.shape, 1)
    vals = logits
    idx_out = jnp.zeros(logits.shape, jnp.int32)
    member = jnp.zeros(logits.shape, F32)
    top, sel = [], []
    for k in range(TOP_K):
        mk = jnp.max(vals, axis=-1, keepdims=True)
        ik = jnp.min(jnp.where(vals == mk, lane, LANES), axis=-1, keepdims=True)
        top.append(mk)
        sel.append(lane == ik)
        idx_out = jnp.where(lane == k, ik, idx_out)
        member = jnp.where(sel[k], 1.0, member)
        vals = jnp.where(sel[k], -jnp.inf, vals)
    es = [jnp.exp(mk - top[0]) for mk in top]
    inv = 1.0 / (es[0] + es[1] + es[2] + es[3])
    gate_out = jnp.zeros(logits.shape, F32)
    for k in range(TOP_K):
        gate_out = jnp.where(lane == k, es[k] * inv, gate_out)
    idx_ref[...] = idx_out
    gate_ref[...] = gate_out

    tm = logits.shape[0]
    earlier = (lax.broadcasted_iota(jnp.int32, (tm, tm), 0)
               > lax.broadcasted_iota(jnp.int32, (tm, tm), 1)).astype(BF16)
    before = jnp.dot(earlier, member.astype(BF16), preferred_element_type=F32) + cnt_ref[0:1, :]
    rank_out = jnp.zeros(logits.shape, jnp.int32)
    for k in range(TOP_K):
        rk = jnp.sum(jnp.where(sel[k], before, 0.0), axis=-1, keepdims=True)
        rank_out = jnp.where(lane == k, rk.astype(jnp.int32), rank_out)
    rank_ref[...] = rank_out
    cnt_ref[...] = cnt_ref[...] + jnp.sum(member, axis=0, keepdims=True)


def _outproj(od, og, x2d, w1, w2, fnw, rw_pad, rb_pad):
    m = x2d.shape[0]
    tm = min(512, m)
    row = lambda i: (i, 0)
    const = lambda i: (0, 0)
    return pl.pallas_call(
        _outproj_kernel,
        out_shape=(jax.ShapeDtypeStruct((m, D_MODEL), F32),
                   jax.ShapeDtypeStruct((m, D_MODEL), F32),
                   jax.ShapeDtypeStruct((m, LANES), jnp.int32),
                   jax.ShapeDtypeStruct((m, LANES), F32),
                   jax.ShapeDtypeStruct((m, LANES), jnp.int32),
                   jax.ShapeDtypeStruct((SUBLANES, LANES), F32)),
        grid=(m // tm,),
        in_specs=[pl.BlockSpec((tm, GROUP_W), row),
                  pl.BlockSpec((tm, GROUP_W), row),
                  pl.BlockSpec((tm, D_MODEL), row),
                  pl.BlockSpec((GROUP_W, D_MODEL), const),
                  pl.BlockSpec((GROUP_W, D_MODEL), const),
                  pl.BlockSpec((1, D_MODEL), const),
                  pl.BlockSpec((D_MODEL, LANES), const),
                  pl.BlockSpec((1, LANES), const)],
        out_specs=(pl.BlockSpec((tm, D_MODEL), row),
                   pl.BlockSpec((tm, D_MODEL), row),
                   pl.BlockSpec((tm, LANES), row),
                   pl.BlockSpec((tm, LANES), row),
                   pl.BlockSpec((tm, LANES), row),
                   pl.BlockSpec((SUBLANES, LANES), const)),
        compiler_params=_cparams(("arbitrary",)),
        name="outproj_router",
    )(od, og, x2d, w1, w2, fnw, rw_pad, rb_pad)


def _cast_kernel(w_ref, o_ref):
    o_ref[...] = w_ref[...].astype(BF16)


def _cast_bf16(w):
    e, r, c = w.shape
    tr = 512
    spec = pl.BlockSpec((1, tr, c), lambda i, j: (i, j, 0))
    return pl.pallas_call(
        _cast_kernel,
        out_shape=jax.ShapeDtypeStruct(w.shape, BF16),
        grid=(e, r // tr),
        in_specs=[spec],
        out_specs=spec,
        compiler_params=_cparams(("arbitrary", "arbitrary")),
        name="cast_bf16",
    )(w)


def _ffn_kernel(be_ref, nu_ref, tok_hbm, h2_hbm, wg_ref, bg_ref, wu_ref, bu_ref, wd_ref, bd_ref, y_ref,
                xa_ref, xb_ref, xbf_ref, ia_ref, ib_ref, isem, rsem, *, nf):
    blk = pl.program_id(0)
    f = pl.program_id(1)
    nu = nu_ref[0]
    slot = blk % 2
    xbufs = (xa_ref, xb_ref)
    idxs = (ia_ref, ib_ref)

    def idx_copy(b, s):
        return pltpu.make_async_copy(tok_hbm.at[b], idxs[s], isem.at[s])

    def issue_rows(s):
        def body(g, carry):
            r0 = pl.multiple_of(g * SUBLANES, SUBLANES)
            for u in range(SUBLANES):
                tok = idxs[s][r0 + u]
                pltpu.make_async_copy(h2_hbm.at[pl.ds(tok, 1)], xbufs[s].at[pl.ds(r0 + u, 1)],
                                      rsem.at[s]).start()
            return carry

        lax.fori_loop(0, EXPERT_BLOCK // SUBLANES, body, 0)

    def wait_rows(s):
        pltpu.make_async_copy(h2_hbm.at[pl.ds(0, EXPERT_BLOCK)], xbufs[s], rsem.at[s]).wait()

    @pl.when(jnp.logical_and(f == 0, blk == 0))
    def _():
        first = idx_copy(0, 0)
        first.start()
        first.wait()
        issue_rows(0)

        @pl.when(1 < nu)
        def _():
            idx_copy(1, 1).start()

    for s in (0, 1):
        @pl.when(jnp.logical_and(jnp.logical_and(f == 0, blk < nu), slot == s))
        def _():
            @pl.when(blk + 1 < nu)
            def _():
                idx_copy(blk + 1, 1 - s).wait()
                issue_rows(1 - s)

            wait_rows(s)
            xbf_ref[...] = xbufs[s][...].astype(BF16)

        @pl.when(jnp.logical_and(jnp.logical_and(f == nf - 1, blk + 2 < nu), slot == s))
        def _():
            idx_copy(blk + 2, s).start()

    @pl.when(blk < nu)
    def _():
        x = xbf_ref[...]
        gate = jnp.minimum(jnp.dot(x, wg_ref[0], preferred_element_type=F32) + bg_ref[0],
                           SWIGLU_LIMIT)
        up = jnp.clip(jnp.dot(x, wu_ref[0], preferred_element_type=F32) + bu_ref[0],
                      -SWIGLU_LIMIT, SWIGLU_LIMIT)
        act = (up + 1.0) * gate * jax.nn.sigmoid(SWIGLU_ALPHA * gate)
        part = jnp.dot(act.astype(BF16), wd_ref[0], preferred_element_type=F32)

        @pl.when(f == 0)
        def _():
            y_ref[...] = part + bd_ref[0]

        @pl.when(f > 0)
        def _():
            y_ref[...] += part

    @pl.when(jnp.logical_and(blk >= nu, f == 0))
    def _():
        y_ref[...] = jnp.zeros(y_ref.shape, F32)


def _moe_ffn(block_e, n_used, slot_tok, h2, wg, bg, wu, bu, wd, bd):
    n_slots = slot_tok.shape[0]
    nb = n_slots // EXPERT_BLOCK
    nf = D_FF // FFN_TF

    def f_of(b, f, nu):
        return jnp.where(b < nu[0], f, nf - 1)

    grid_spec = pltpu.PrefetchScalarGridSpec(
        num_scalar_prefetch=2,
        grid=(nb, nf),
        in_specs=[pl.BlockSpec(memory_space=pl.ANY),
                  pl.BlockSpec(memory_space=pl.ANY),
                  pl.BlockSpec((1, D_MODEL, FFN_TF), lambda b, f, be, nu: (be[b], 0, f_of(b, f, nu))),
                  pl.BlockSpec((1, 1, FFN_TF), lambda b, f, be, nu: (be[b], 0, f_of(b, f, nu))),
                  pl.BlockSpec((1, D_MODEL, FFN_TF), lambda b, f, be, nu: (be[b], 0, f_of(b, f, nu))),
                  pl.BlockSpec((1, 1, FFN_TF), lambda b, f, be, nu: (be[b], 0, f_of(b, f, nu))),
                  pl.BlockSpec((1, FFN_TF, D_MODEL), lambda b, f, be, nu: (be[b], f_of(b, f, nu), 0)),
                  pl.BlockSpec((1, 1, D_MODEL), lambda b, f, be, nu: (be[b], 0, 0))],
        out_specs=pl.BlockSpec((EXPERT_BLOCK, D_MODEL), lambda b, f, be, nu: (b, 0)),
        scratch_shapes=[pltpu.VMEM((EXPERT_BLOCK, D_MODEL), F32),
                        pltpu.VMEM((EXPERT_BLOCK, D_MODEL), F32),
                        pltpu.VMEM((EXPERT_BLOCK, D_MODEL), BF16),
                        pltpu.SMEM((EXPERT_BLOCK,), jnp.int32),
                        pltpu.SMEM((EXPERT_BLOCK,), jnp.int32),
                        pltpu.SemaphoreType.DMA((2,)),
                        pltpu.SemaphoreType.DMA((2,))],
    )
    return pl.pallas_call(
        functools.partial(_ffn_kernel, nf=nf),
        out_shape=jax.ShapeDtypeStruct((n_slots, D_MODEL), F32),
        grid_spec=grid_spec,
        compiler_params=_cparams(("arbitrary", "arbitrary")),
        name="moe_ffn",
    )(block_e, n_used, slot_tok.reshape(nb, EXPERT_BLOCK), h2, wg, bg, wu, bu, wd, bd)


def _combine_kernel(pos_hbm, ys_hbm, x1_ref, gate_ref, fnw_ref, y_ref, pa_ref, pb_ref, ya_ref, yb_ref,
                    isem, rsem, *, tm, nb):
    blk = pl.program_id(0)
    slot = blk % 2
    poss = (pa_ref, pb_ref)
    ybufs = (ya_ref, yb_ref)

    def idx_copy(b, s):
        return pltpu.make_async_copy(pos_hbm.at[b], poss[s], isem.at[s])

    def issue_rows(s):
        def body(g, carry):
            t0 = pl.multiple_of(g * SUBLANES, SUBLANES)
            for u in range(SUBLANES):
                for k in range(TOP_K):
                    p = poss[s][(t0 + u) * TOP_K + k]
                    pltpu.make_async_copy(ys_hbm.at[pl.ds(p, 1)], ybufs[s].at[k, pl.ds(t0 + u, 1)],
                                          rsem.at[s]).start()
            return carry

        lax.fori_loop(0, tm // SUBLANES, body, 0)

    def wait_rows(s):
        for k in range(TOP_K):
            pltpu.make_async_copy(ys_hbm.at[pl.ds(0, tm)], ybufs[s].at[k], rsem.at[s]).wait()

    @pl.when(blk == 0)
    def _():
        first = idx_copy(0, 0)
        first.start()
        first.wait()
        issue_rows(0)
        if nb > 1:
            idx_copy(1, 1).start()

    for s in (0, 1):
        @pl.when(jnp.logical_and(blk + 1 < nb, slot == s))
        def _():
            idx_copy(blk + 1, 1 - s).wait()
            issue_rows(1 - s)

        @pl.when(jnp.logical_and(blk + 2 < nb, slot == s))
        def _():
            idx_copy(blk + 2, s).start()

        @pl.when(slot == s)
        def _():
            wait_rows(s)
            g = gate_ref[...]
            moe = g[:, 0:1] * ybufs[s][0]
            for k in range(1, TOP_K):
                moe = moe + g[:, k:k + 1] * ybufs[s][k]
            x2 = x1_ref[...] + moe
            y_ref[...] = (x2 * lax.rsqrt(jnp.mean(x2 * x2, axis=-1, keepdims=True) + EPS)
                          * fnw_ref[...])


def _combine(pos, y_slots, x1, gates, final_norm_w):
    m = x1.shape[0]
    tm = min(COMBINE_TM, m)
    nb = m // tm
    row = lambda i: (i, 0)
    return pl.pallas_call(
        functools.partial(_combine_kernel, tm=tm, nb=nb),
        out_shape=jax.ShapeDtypeStruct((m, D_MODEL), F32),
        grid=(nb,),
        in_specs=[pl.BlockSpec(memory_space=pl.ANY),
                  pl.BlockSpec(memory_space=pl.ANY),
                  pl.BlockSpec((tm, D_MODEL), row),
                  pl.BlockSpec((tm, LANES), row),
                  pl.BlockSpec((1, D_MODEL), lambda i: (0, 0))],
        out_specs=pl.BlockSpec((tm, D_MODEL), row),
        scratch_shapes=[pltpu.SMEM((tm * TOP_K,), jnp.int32),
                        pltpu.SMEM((tm * TOP_K,), jnp.int32),
                        pltpu.VMEM((TOP_K, tm, D_MODEL), F32),
                        pltpu.VMEM((TOP_K, tm, D_MODEL), F32),
                        pltpu.SemaphoreType.DMA((2,)),
                        pltpu.SemaphoreType.DMA((2,))],
        compiler_params=_cparams(("arbitrary",)),
        name="moe_combine",
    )(pos.reshape(nb, tm * TOP_K), y_slots, x1, gates, final_norm_w.reshape(1, D_MODEL))


def _route(streams):
    experts = jnp.arange(N_EXPERTS, dtype=jnp.int32)
    counts = sum(c for _, _, c in streams)
    padded = (counts + EXPERT_BLOCK - 1) // EXPERT_BLOCK * EXPERT_BLOCK
    padded_end = jnp.cumsum(padded)
    base = padded_end - padded
    dests = []
    for top_idx, rank, c in streams:
        onehot = (top_idx[:, :, None] == experts[None, None, :]).astype(jnp.int32)
        dests.append((jnp.sum(onehot * base[None, None, :], axis=-1) + rank).reshape(-1).astype(jnp.int32))
        base = base + c
    n_assign = sum(d.shape[0] for d in dests)
    n_blocks = -(-n_assign // EXPERT_BLOCK) + N_EXPERTS
    flat_tok = jnp.arange(n_assign, dtype=jnp.int32) // TOP_K
    slot_tok = jnp.zeros((n_blocks * EXPERT_BLOCK,), jnp.int32).at[jnp.concatenate(dests)].set(flat_tok)
    starts = jnp.arange(n_blocks, dtype=jnp.int32) * EXPERT_BLOCK
    block_e = jnp.minimum(jnp.sum((padded_end[None, :] <= starts[:, None]).astype(jnp.int32), axis=1),
                          N_EXPERTS - 1).astype(jnp.int32)
    n_used = (padded_end[-1] // EXPERT_BLOCK).astype(jnp.int32).reshape(1)
    return slot_tok, dests, block_e, n_used


def _mixer(x, pos, k_cache, v_cache, conv_past, s0, lam, attn_norm_w, w_perm, conv_w_t, neg_a, dt_b,
           diff_norm_w, gdn_norm_w):
    b, t, _ = x.shape
    m = b * t
    tm = min(1024, m)
    tabs = _rope_tables(pos)
    if t < tm:
        tabs = tuple(jnp.tile(a, (tm // t, 1)) for a in tabs)
    proj, qkv = _inproj(x.reshape(m, D_MODEL), attn_norm_w, w_perm, tabs)
    proj3 = proj.reshape(b, t, PROJ_COLS)
    qkv3 = qkv.reshape(b, t, 3 * GROUP_W)
    k_rows = proj3[:, :, GROUP_W:2 * GROUP_W].reshape(b, t, N_HEADS, HEAD)
    v_rows = proj3[:, :, 2 * GROUP_W:3 * GROUP_W].reshape(b, t, N_HEADS, HEAD)

    if k_cache is None:
        tk = min(512, t // 4)
        vt = qkv3[:, :, 2 * GROUP_W:].reshape(b, t // tk, tk, N_HEADS, HEAD).transpose(0, 3, 1, 4, 2)
        o_d = _diff_attn_prompt(lam, qkv3, vt, diff_norm_w, tk=tk)
    else:
        past = k_cache.shape[1]
        k_all = jnp.concatenate([k_cache.reshape(b, past, GROUP_W).astype(BF16),
                                 qkv3[:, :, GROUP_W:2 * GROUP_W]], axis=1)
        v_all = jnp.concatenate([v_cache.reshape(b, past, N_HEADS, HEAD).astype(BF16),
                                 qkv3[:, :, 2 * GROUP_W:].reshape(b, t, N_HEADS, HEAD)], axis=1)
        vt = v_all.transpose(0, 2, 3, 1)[:, :, None]
        o_d = _diff_attn_full(lam, qkv3, k_all, vt, diff_norm_w)

    conv_past8 = jnp.pad(conv_past, ((0, 0), (SUBLANES - (CONV_K - 1), 0), (0, 0)))
    o_g, s_new, conv8 = _gdn(proj3, conv_past8, s0, conv_w_t, neg_a, dt_b, gdn_norm_w)
    conv_new = conv8[:, SUBLANES - (CONV_K - 1):, :]
    return o_d.reshape(m, GROUP_W), o_g.reshape(m, GROUP_W), k_rows, v_rows, conv_new, s_new


def kernel(x_prompt, x_sample, cache_k_diff, cache_v_diff, state_conv, state_gdn, attn_norm_w, w_in,
           conv_w, a_log, dt_bias, lambda_q1, lambda_k1, lambda_q2, lambda_k2, diff_norm_w, gdn_norm_w,
           w_out, ffn_norm_w, router_w, router_b, w_gate, b_gate, w_up, b_up, w_down, b_down,
           final_norm_w):
    l = 0
    bp, tp, _ = x_prompt.shape
    bs, ts, _ = x_sample.shape
    past_len = cache_k_diff.shape[2]

    lam = (jnp.exp(jnp.sum(lambda_q1[l] * lambda_k1[l])) - jnp.exp(jnp.sum(lambda_q2[l] * lambda_k2[l]))
           + LAM_INIT).astype(F32).reshape(1, 1)

    wl = w_in[l]
    ncol = wl.shape[1]
    w_perm = jnp.concatenate(
        [wl[:, :COL_Z], wl[:, ncol - GROUP_W:], wl[:, COL_Z:COL_Z + 2 * N_HEADS],
         jnp.zeros((D_MODEL, PROJ_COLS - ncol), F32)], axis=1).astype(BF16)
    conv_w_t = conv_w[l].T
    pad8 = lambda a: jnp.pad(a.astype(F32), (0, LANES - N_HEADS)).reshape(1, LANES)
    neg_a = pad8(-jnp.exp(a_log[l]))
    dt_b = pad8(dt_bias[l])
    w1 = w_out[l][:GROUP_W].astype(BF16)
    w2 = w_out[l][GROUP_W:].astype(BF16)
    rw_pad = jnp.pad(router_w[l], ((0, 0), (0, LANES - N_EXPERTS)))
    rb_pad = jnp.concatenate([router_b[l].astype(F32),
                              jnp.full((LANES - N_EXPERTS,), -jnp.inf, F32)]).reshape(1, LANES)
    wg = _cast_bf16(w_gate[l])
    wu = _cast_bf16(w_up[l])
    wd = _cast_bf16(w_down[l])
    bg = b_gate[l].reshape(N_EXPERTS, 1, D_FF)
    bu = b_up[l].reshape(N_EXPERTS, 1, D_FF)
    bd = b_down[l].reshape(N_EXPERTS, 1, D_MODEL)

    mix_args = (lam, attn_norm_w[l], w_perm, conv_w_t, neg_a, dt_b, diff_norm_w[l], gdn_norm_w[l])
    conv0 = jnp.zeros((bp, CONV_K - 1, 3 * GROUP_W), F32)
    s0 = jnp.zeros((bp, N_HEADS, HEAD, HEAD), F32)
    od_p, og_p, kp, vp, cp, sp = _mixer(x_prompt, jnp.arange(tp), None, None, conv0, s0, *mix_args)
    od_s, og_s, ks, vs, cs, ss = _mixer(x_sample, past_len + jnp.arange(ts), cache_k_diff[l],
                                        cache_v_diff[l], state_conv[l], state_gdn[l], *mix_args)

    fnw = ffn_norm_w[l].reshape(1, D_MODEL)
    x1_p, h2_p, idx_p, gate_p, rank_p, cnt_p = _outproj(
        od_p, og_p, x_prompt.reshape(bp * tp, D_MODEL), w1, w2, fnw, rw_pad, rb_pad)
    x1_s, h2_s, idx_s, gate_s, rank_s, cnt_s = _outproj(
        od_s, og_s, x_sample.reshape(bs * ts, D_MODEL), w1, w2, fnw, rw_pad, rb_pad)

    h2 = jnp.concatenate([h2_p, h2_s], axis=0)
    counts = lambda c: c[0, :N_EXPERTS].astype(jnp.int32)
    slot_tok, (dest_p, dest_s), block_e, n_used = _route(
        [(idx_p[:, :TOP_K], rank_p[:, :TOP_K], counts(cnt_p)),
         (idx_s[:, :TOP_K], rank_s[:, :TOP_K], counts(cnt_s))])
    y_slots = _moe_ffn(block_e, n_used, slot_tok, h2, wg, bg, wu, bu, wd, bd)
    y_p = _combine(dest_p, y_slots, x1_p, gate_p, final_norm_w)
    y_s = _combine(dest_s, y_slots, x1_s, gate_s, final_norm_w)

    return (y_p.reshape(bp, tp, D_MODEL), y_s.reshape(bs, ts, D_MODEL),
            kp[None], vp[None], cp[None], sp[None], ks[None], vs[None], cs[None], ss[None])
```

```python
import functools
import math

import jax
import jax.numpy as jnp
from jax import lax
from jax.experimental import pallas as pl
from jax.experimental.pallas import tpu as pltpu

F32 = jnp.float32
BF16 = jnp.bfloat16

D_MODEL = 2048
N_HEADS = 8
HEAD = 128
QK_DIM = 64
ROT_DIM = 16
ROT_HALF = ROT_DIM // 2
ROPE_THETA = 500000.0
CHUNK = 64
GROUP_W = N_HEADS * HEAD
CONV_K = 4
N_EXPERTS = 32
TOP_K = 4
D_FF = 2048
SWIGLU_LIMIT = 7.0
SWIGLU_ALPHA = 1.702
EPS = 1e-6
SUBLN_EPS = 1e-5
LAM_INIT = 0.8 - 0.6 * math.exp(-0.3 * 0)
Q_SCALE = QK_DIM ** -0.5 * math.log2(math.e)

LANES = 128
SUBLANES = 8
VMEM_LIMIT = 56 * 1024 * 1024

PROJ_TN = 512
PROJ_COLS = 7680
COL_Z = 6 * GROUP_W
REST_COLS = PROJ_COLS - 3 * GROUP_W
REST_Z = 3 * GROUP_W
REST_AB = 4 * GROUP_W

EXPERT_BLOCK = 512
FFN_TF = 1024
COMBINE_TM = 256
GDN_BATCH = 2


def _cparams(sem, vmem=VMEM_LIMIT):
    return pltpu.CompilerParams(dimension_semantics=sem, vmem_limit_bytes=vmem)


def _inproj_kernel(x_ref, nw_ref, w_ref, c_ref, s1_ref, s2_ref, k_ref, v_ref, rest_ref, qkv_ref, h_ref):
    j = pl.program_id(1)

    @pl.when(j == 0)
    def _():
        x = x_ref[...]
        ms = jnp.mean(x * x, axis=-1, keepdims=True)
        h_ref[...] = (x * lax.rsqrt(ms + EPS) * nw_ref[...]).astype(BF16)

    acc = jnp.dot(h_ref[...], w_ref[...], preferred_element_type=F32)

    def rope(g):
        xg = acc[:, g * LANES:(g + 1) * LANES]
        return (xg * c_ref[...] + pltpu.roll(xg, LANES - ROT_HALF, 1) * s1_ref[...]
                + pltpu.roll(xg, ROT_HALF, 1) * s2_ref[...])

    @pl.when(j < 2)
    def _():
        for g in range(PROJ_TN // LANES):
            qkv_ref[:, g * LANES:(g + 1) * LANES] = (rope(g) * Q_SCALE).astype(BF16)

    @pl.when(jnp.logical_and(j >= 2, j < 4))
    def _():
        for g in range(PROJ_TN // LANES):
            r = rope(g)
            k_ref[:, g * LANES:(g + 1) * LANES] = r
            qkv_ref[:, g * LANES:(g + 1) * LANES] = r.astype(BF16)

    @pl.when(jnp.logical_and(j >= 4, j < 6))
    def _():
        v_ref[...] = acc
        qkv_ref[...] = acc.astype(BF16)

    @pl.when(j >= 6)
    def _():
        rest_ref[...] = acc


def _rope_tables(pos):
    inv_freq = jnp.power(ROPE_THETA, -jnp.arange(ROT_HALF, dtype=F32) * (2.0 / ROT_DIM))
    ang = pos.astype(F32)[:, None] * inv_freq[None, :]
    cos, sin = jnp.cos(ang), jnp.sin(ang)
    t = pos.shape[0]
    one = jnp.ones((t, QK_DIM - ROT_DIM), F32)
    zero = jnp.zeros((t, QK_DIM - ROT_DIM), F32)
    z8 = jnp.zeros((t, ROT_HALF), F32)
    c = jnp.concatenate([cos, cos, one], axis=1)
    s1 = jnp.concatenate([-sin, z8, zero], axis=1)
    s2 = jnp.concatenate([z8, sin, zero], axis=1)
    return tuple(jnp.concatenate([a, a], axis=1) for a in (c, s1, s2))


def _inproj(x2d, norm_w, w_perm, tables):
    m = x2d.shape[0]
    tm = min(1024, m)
    tab_rows = tables[0].shape[0]
    n_tab = tab_rows // tm
    grid = (m // tm, PROJ_COLS // PROJ_TN)
    tab_spec = pl.BlockSpec((tm, LANES), lambda i, j: (i % n_tab, 0))
    return pl.pallas_call(
        _inproj_kernel,
        out_shape=(jax.ShapeDtypeStruct((m, GROUP_W), F32),
                   jax.ShapeDtypeStruct((m, GROUP_W), F32),
                   jax.ShapeDtypeStruct((m, REST_COLS), F32),
                   jax.ShapeDtypeStruct((m, 3 * GROUP_W), BF16)),
        grid=grid,
        in_specs=[pl.BlockSpec((tm, D_MODEL), lambda i, j: (i, 0)),
                  pl.BlockSpec((1, D_MODEL), lambda i, j: (0, 0)),
                  pl.BlockSpec((D_MODEL, PROJ_TN), lambda i, j: (0, j)),
                  tab_spec, tab_spec, tab_spec],
        out_specs=(pl.BlockSpec((tm, PROJ_TN), lambda i, j: (i, jnp.clip(j - 2, 0, 1))),
                   pl.BlockSpec((tm, PROJ_TN), lambda i, j: (i, jnp.clip(j - 4, 0, 1))),
                   pl.BlockSpec((tm, PROJ_TN), lambda i, j: (i, jnp.maximum(j - 6, 0))),
                   pl.BlockSpec((tm, PROJ_TN), lambda i, j: (i, jnp.minimum(j, 5)))),
        scratch_shapes=[pltpu.VMEM((tm, D_MODEL), BF16)],
        compiler_params=_cparams(("arbitrary", "arbitrary")),
        name="inproj",
    )(x2d, norm_w.reshape(1, D_MODEL), w_perm, *tables)


def _split_maps(q_ref, qq_ref, tq):
    q = q_ref[0]
    lane = lax.broadcasted_iota(jnp.int32, (tq, LANES), 1)
    zero = jnp.zeros_like(q)
    qq_ref[0:tq, :] = jnp.where(lane < QK_DIM, q, zero)
    qq_ref[tq:2 * tq, :] = jnp.where(lane >= QK_DIM, q, zero)


def _attn_finish(lam_ref, nw_ref, o_ref, l_ref, acc_ref, tq):
    lam = lam_ref[0, 0]
    inv_l = 1.0 / l_ref[...]
    o = acc_ref[...] * inv_l
    o = o[:, 0:tq] - lam * o[:, tq:2 * tq]
    ms = jnp.mean(o * o, axis=0, keepdims=True)
    on = (o * lax.rsqrt(ms + SUBLN_EPS)).T
    o_ref[0] = ((on * nw_ref[...]) * (1.0 - LAM_INIT)).astype(BF16)


def _attn_full_kernel(lam_ref, q_ref, k_ref, vt_ref, nw_ref, o_ref, qq_ref, l_ref, acc_ref, *, tq):
    _split_maps(q_ref, qq_ref, tq)
    s = lax.dot_general(k_ref[0], qq_ref[...], (((1,), (1,)), ((), ())),
                        preferred_element_type=F32)
    p = jnp.exp2(s - jnp.max(s, axis=0, keepdims=True))
    l_ref[...] = jnp.sum(p, axis=0, keepdims=True)
    acc_ref[...] = jnp.dot(vt_ref[0, 0, 0], p.astype(BF16), preferred_element_type=F32)
    _attn_finish(lam_ref, nw_ref, o_ref, l_ref, acc_ref, tq)


def _diff_attn_full(lam, q_src, k_all, vt, norm_w):
    b, tq = q_src.shape[0], q_src.shape[1]
    t_k = k_all.shape[1]
    return pl.pallas_call(
        functools.partial(_attn_full_kernel, tq=tq),
        out_shape=jax.ShapeDtypeStruct((b, tq, GROUP_W), BF16),
        grid=(b, N_HEADS),
        in_specs=[pl.BlockSpec(memory_space=pltpu.SMEM),
                  pl.BlockSpec((1, tq, HEAD), lambda bi, h: (bi, 0, h)),
                  pl.BlockSpec((1, t_k, HEAD), lambda bi, h: (bi, 0, h)),
                  pl.BlockSpec((1, 1, 1, HEAD, t_k), lambda bi, h: (bi, h, 0, 0, 0)),
                  pl.BlockSpec((1, HEAD), lambda bi, h: (0, 0))],
        out_specs=pl.BlockSpec((1, tq, HEAD), lambda bi, h: (bi, 0, h)),
        scratch_shapes=[pltpu.VMEM((2 * tq, HEAD), BF16),
                        pltpu.VMEM((1, 2 * tq), F32),
                        pltpu.VMEM((HEAD, 2 * tq), F32)],
        compiler_params=_cparams(("arbitrary", "arbitrary")),
        name="diff_attn_full",
    )(lam, q_src, k_all, vt, norm_w.reshape(1, HEAD))


def _attn_pipe_kernel(lam_ref, q_ref, k_ref, vt_ref, nw_ref, o_ref,
                      qq_ref, sa_ref, sb_ref, pa_ref, pb_ref, ala_ref, alb_ref, ma_ref, mb_ref, m_ref, l_ref,
                      acc_ref, *, tq, tk):
    i = pl.program_id(2)
    _split_maps(q_ref, qq_ref, tq)
    m_ref[...] = jnp.full(m_ref.shape, -jnp.inf, F32)
    l_ref[...] = jnp.zeros(l_ref.shape, F32)
    acc_ref[...] = jnp.zeros(acc_ref.shape, F32)
    s_bufs = (sa_ref, sb_ref)
    p_bufs = (pa_ref, pb_ref)
    al_bufs = (ala_ref, alb_ref)
    mx_bufs = (ma_ref, mb_ref)

    def scores(j, par, masked):
        kb = k_ref[0, pl.ds(pl.multiple_of(j * tk, tk), tk), :]
        s = lax.dot_general(kb, qq_ref[...], (((1,), (1,)), ((), ())), preferred_element_type=F32)
        if masked:
            kchunk = (j * tk + lax.broadcasted_iota(jnp.int32, (tk, 2 * tq), 0)) // CHUNK
            qcol = lax.broadcasted_iota(jnp.int32, (tk, 2 * tq), 1) & (tq - 1)
            s = jnp.where(kchunk <= (i * tq + qcol) // CHUNK, s, -jnp.inf)
        s_bufs[par][...] = s
        mx_bufs[par][...] = jnp.max(s, axis=0, keepdims=True)

    def softmax(par):
        s = s_bufs[par][...]
        m_old = m_ref[...]
        m_new = jnp.maximum(m_old, mx_bufs[par][...])
        alpha = jnp.exp2(m_old - m_new)
        p = jnp.exp2(s - m_new)
        l_ref[...] = alpha * l_ref[...] + jnp.sum(p, axis=0, keepdims=True)
        m_ref[...] = m_new
        p_bufs[par][...] = p.astype(BF16)
        al_bufs[par][...] = alpha

    def pv(j, par):
        upd = jnp.dot(vt_ref[0, 0, j], p_bufs[par][...], preferred_element_type=F32)
        acc_ref[...] = acc_ref[...] * al_bufs[par][...] + upd

    n = 2 * i + 2
    scores(0, 0, True)
    scores(1, 1, True)
    softmax(0)

    def pair(u, carry):
        t = 2 * u + 1
        scores(t + 1, 0, False)
        softmax(1)
        pv(t - 1, 0)
        scores(t + 2, 1, False)
        softmax(0)
        pv(t, 1)
        return carry

    lax.fori_loop(0, i - 1, pair, 0)

    @pl.when(i > 0)
    def _():
        t = n - 3
        scores(t + 1, 0, True)
        softmax(1)
        pv(t - 1, 0)
        scores(t + 2, 1, True)
        softmax(0)
        pv(t, 1)

    softmax(1)
    pv(n - 2, 0)
    pv(n - 1, 1)
    _attn_finish(lam_ref, nw_ref, o_ref, l_ref, acc_ref, tq)


def _diff_attn_prompt(lam, qkv3, vt, norm_w, *, tk):
    b, t = qkv3.shape[0], qkv3.shape[1]
    tq = 2 * tk
    nkb = t // tk
    kern = functools.partial(_attn_pipe_kernel, tq=tq, tk=tk)
    stat = pltpu.VMEM((1, 2 * tq), F32)
    return pl.pallas_call(
        kern,
        out_shape=jax.ShapeDtypeStruct((b, t, GROUP_W), BF16),
        grid=(b, N_HEADS, t // tq),
        in_specs=[pl.BlockSpec(memory_space=pltpu.SMEM),
                  pl.BlockSpec((1, tq, HEAD), lambda bi, h, i: (bi, i, h)),
                  pl.BlockSpec((1, t, HEAD), lambda bi, h, i: (bi, 0, N_HEADS + h)),
                  pl.BlockSpec((1, 1, nkb, HEAD, tk), lambda bi, h, i: (bi, h, 0, 0, 0)),
                  pl.BlockSpec((1, HEAD), lambda bi, h, i: (0, 0))],
        out_specs=pl.BlockSpec((1, tq, HEAD), lambda bi, h, i: (bi, i, h)),
        scratch_shapes=[pltpu.VMEM((2 * tq, HEAD), BF16),
                        pltpu.VMEM((tk, 2 * tq), F32),
                        pltpu.VMEM((tk, 2 * tq), F32),
                        pltpu.VMEM((tk, 2 * tq), BF16),
                        pltpu.VMEM((tk, 2 * tq), BF16),
                        stat, stat, stat, stat, stat, stat,
                        pltpu.VMEM((HEAD, 2 * tq), F32)],
        compiler_params=_cparams(("arbitrary", "arbitrary", "arbitrary")),
        name="diff_attn_prompt",
    )(lam, qkv3, qkv3, vt, norm_w.reshape(1, HEAD))


def _bdot(a, b):
    return jnp.dot(a.astype(BF16), b.astype(BF16), preferred_element_type=F32)


def _bdot_nt(a, b):
    return lax.dot_general(a.astype(BF16), b.astype(BF16), (((1,), (1,)), ((), ())),
                           preferred_element_type=F32)


def _bdot_tn(a, b):
    return lax.dot_general(a.astype(BF16), b.astype(BF16), (((0,), (0,)), ((), ())),
                           preferred_element_type=F32)


def _split3(a):
    hi = a.astype(BF16)
    r = a - hi.astype(F32)
    mid = r.astype(BF16)
    lo = (r - mid.astype(F32)).astype(BF16)
    return hi, mid, lo


def _dot_hp(a, b):
    a_hi = a.astype(BF16)
    a_lo = (a - a_hi.astype(F32)).astype(BF16)
    b_hi = b.astype(BF16)
    b_lo = (b - b_hi.astype(F32)).astype(BF16)
    d = functools.partial(jnp.dot, preferred_element_type=F32)
    return d(a_hi, b_hi) + (d(a_hi, b_lo) + d(a_lo, b_hi))


def _gdn_kernel(qkv_ref, z_ref, ab_ref, convp_ref, s0_ref, cw_ref, alog_ref, dtb_ref, gnw_ref,
                og_ref, sfin_ref, convn_ref, xbuf_ref, s_ref, *, nc, nb):
    c = pl.program_id(1)

    @pl.when(c == 0)
    def _():
        xbuf_ref[:, 0:SUBLANES, :] = convp_ref[...]
        s_ref[...] = s0_ref[...]

    xbuf_ref[:, SUBLANES:SUBLANES + CHUNK, :] = qkv_ref[...]

    row = lax.broadcasted_iota(jnp.int32, (CHUNK, CHUNK), 0)
    col = lax.broadcasted_iota(jnp.int32, (CHUNK, CHUNK), 1)
    incl = row >= col
    strict = row > col
    blk = (row // 16) == (col // 16)
    eye = (row == col).astype(F32)
    tri = incl.astype(BF16)
    d = functools.partial(jnp.dot, preferred_element_type=F32)
    S = [(bb, h) for bb in range(nb) for h in range(N_HEADS)]

    gc_all, gc_t, beta_all = {}, {}, {}
    for bb in range(nb):
        ab = ab_ref[bb]
        g_all = alog_ref[...] * jax.nn.softplus(ab + dtb_ref[...])
        g_hi, g_mid, g_lo = _split3(g_all)
        gc_all[bb] = d(tri, g_hi) + (d(tri, g_mid) + d(tri, g_lo))
        gc_t[bb] = gc_all[bb].T
        beta_all[bb] = jax.nn.sigmoid(ab)

    def conv_silu(bb, col0):
        cs = slice(col0, col0 + HEAD)
        y = xbuf_ref[bb, 5:5 + CHUNK, cs] * cw_ref[0:1, cs]
        y = y + xbuf_ref[bb, 6:6 + CHUNK, cs] * cw_ref[1:2, cs]
        y = y + xbuf_ref[bb, 7:7 + CHUNK, cs] * cw_ref[2:3, cs]
        y = y + xbuf_ref[bb, 8:8 + CHUNK, cs] * cw_ref[3:4, cs]
        return y * jax.nn.sigmoid(y)

    qr = {k: conv_silu(k[0], k[1] * HEAD) for k in S}
    kr = {k: conv_silu(k[0], GROUP_W + k[1] * HEAD) for k in S}
    v = {k: conv_silu(k[0], 2 * GROUP_W + k[1] * HEAD) for k in S}
    qn = {k: qr[k] * lax.rsqrt(jnp.sum(qr[k] * qr[k], axis=-1, keepdims=True) + 1e-6) * (HEAD ** -0.5)
          for k in S}
    kn = {k: kr[k] * lax.rsqrt(jnp.sum(kr[k] * kr[k], axis=-1, keepdims=True) + 1e-6) for k in S}
    beta = {k: beta_all[k[0]][:, SUBLANES + k[1]:SUBLANES + k[1] + 1] for k in S}
    gc_col = {k: gc_all[k[0]][:, k[1]:k[1] + 1] for k in S}
    g_last = {k: gc_all[k[0]][CHUNK - 1:CHUNK, k[1]:k[1] + 1] for k in S}
    decay = {k: jnp.where(incl, jnp.exp(jnp.where(incl, gc_col[k] - gc_t[k[0]][k[1]:k[1] + 1, :], 0.0)), 0.0)
             for k in S}
    kb = {k: kn[k] * beta[k] for k in S}
    a = {k: jnp.where(strict, _bdot_nt(kb[k], kn[k]) * decay[k], 0.0) for k in S}
    qk = {k: jnp.where(incl, _bdot_nt(qn[k], kn[k]) * decay[k], 0.0) for k in S}
    eg = {k: jnp.exp(gc_col[k]) for k in S}
    rhs = {k: jnp.concatenate([v[k] * beta[k], kb[k] * eg[k]], axis=1) for k in S}

    ad = {k: jnp.where(blk, a[k], 0.0) for k in S}
    ao = {k: jnp.where(blk, 0.0, a[k]) for k in S}
    a2 = {k: _bdot(ad[k], ad[k]) for k in S}
    a4 = {k: _bdot(a2[k], a2[k]) for k in S}
    a8 = {k: _bdot(a4[k], a4[k]) for k in S}
    p = {k: eye - ad[k] for k in S}
    p = {k: p[k] + _bdot(p[k], a2[k]) for k in S}
    p = {k: p[k] + _bdot(p[k], a4[k]) for k in S}
    p = {k: p[k] + _bdot(p[k], a8[k]) for k in S}
    y = {k: _bdot(p[k], rhs[k]) for k in S}
    bm = {k: _bdot(p[k], ao[k]) for k in S}
    b2 = {k: _bdot(bm[k], bm[k]) for k in S}
    zz = {k: y[k] + _bdot(b2[k], y[k]) for k in S}
    sol = {k: zz[k] - _bdot(bm[k], zz[k]) for k in S}

    s_old = {k: s_ref[k[0], k[1]] for k in S}
    v_new = {k: sol[k][:, 0:HEAD] - _bdot(sol[k][:, HEAD:2 * HEAD], s_old[k]) for k in S}
    o = {k: _bdot(qn[k] * eg[k], s_old[k]) + _bdot(qk[k], v_new[k]) for k in S}
    k_dec = {k: kn[k] * jnp.exp(g_last[k] - gc_col[k]) for k in S}
    for k in S:
        s_ref[k[0], k[1]] = s_old[k] * jnp.exp(g_last[k]) + _bdot_tn(k_dec[k], v_new[k])
    for k in S:
        bb, h = k
        on = o[k] * lax.rsqrt(jnp.mean(o[k] * o[k], axis=-1, keepdims=True) + EPS) * gnw_ref[...]
        zh = z_ref[bb, :, h * HEAD:(h + 1) * HEAD]
        og_ref[bb, :, h * HEAD:(h + 1) * HEAD] = (on * (zh * jax.nn.sigmoid(zh))).astype(BF16)

    xbuf_ref[:, 0:SUBLANES, :] = xbuf_ref[:, CHUNK:CHUNK + SUBLANES, :]

    @pl.when(c == nc - 1)
    def _():
        sfin_ref[...] = s_ref[...]
        convn_ref[...] = xbuf_ref[:, CHUNK:CHUNK + SUBLANES, :]


def _gdn(proj3, conv_past8, s0, conv_w_t, neg_a, dt_bias, gnw):
    b, t = proj3.shape[0], proj3.shape[1]
    nc = t // CHUNK
    nb = GDN_BATCH if b % GDN_BATCH == 0 else 1
    w3 = 3 * GROUP_W
    return pl.pallas_call(
        functools.partial(_gdn_kernel, nc=nc, nb=nb),
        out_shape=(jax.ShapeDtypeStruct((b, t, GROUP_W), BF16),
                   jax.ShapeDtypeStruct((b, N_HEADS, HEAD, HEAD), F32),
                   jax.ShapeDtypeStruct((b, SUBLANES, w3), F32)),
        grid=(b // nb, nc),
        in_specs=[pl.BlockSpec((nb, CHUNK, w3), lambda bi, c: (bi, c, 0)),
                  pl.BlockSpec((nb, CHUNK, GROUP_W), lambda bi, c: (bi, c, REST_Z // GROUP_W)),
                  pl.BlockSpec((nb, CHUNK, LANES), lambda bi, c: (bi, c, REST_AB // LANES)),
                  pl.BlockSpec((nb, SUBLANES, w3), lambda bi, c: (bi, 0, 0)),
                  pl.BlockSpec((nb, N_HEADS, HEAD, HEAD), lambda bi, c: (bi, 0, 0, 0)),
                  pl.BlockSpec((CONV_K, w3), lambda bi, c: (0, 0)),
                  pl.BlockSpec((1, LANES), lambda bi, c: (0, 0)),
                  pl.BlockSpec((1, LANES), lambda bi, c: (0, 0)),
                  pl.BlockSpec((1, HEAD), lambda bi, c: (0, 0))],
        out_specs=(pl.BlockSpec((nb, CHUNK, GROUP_W), lambda bi, c: (bi, c, 0)),
                   pl.BlockSpec((nb, N_HEADS, HEAD, HEAD), lambda bi, c: (bi, 0, 0, 0)),
                   pl.BlockSpec((nb, SUBLANES, w3), lambda bi, c: (bi, 0, 0))),
        scratch_shapes=[pltpu.VMEM((nb, CHUNK + SUBLANES, w3), F32),
                        pltpu.VMEM((nb, N_HEADS, HEAD, HEAD), F32)],
        compiler_params=_cparams(("arbitrary", "arbitrary")),
        name="gdn",
    )(proj3, proj3, proj3, conv_past8, s0, conv_w_t, neg_a, dt_bias, gnw.reshape(1, HEAD))


def _outproj_kernel(od_ref, og_ref, x_ref, w1_ref, w2_ref, fnw_ref, rw_ref, rb_ref,
                    x1_ref, h2_ref, idx_ref, gate_ref, rank_ref, cnt_ref):
    @pl.when(pl.program_id(0) == 0)
    def _():
        cnt_ref[...] = jnp.zeros(cnt_ref.shape, F32)

    acc = jnp.dot(od_ref[...], w1_ref[...], preferred_element_type=F32)
    acc = acc + jnp.dot(og_ref[...], w2_ref[...], preferred_element_type=F32)
    x1 = x_ref[...] + acc
    x1_ref[...] = x1
    h = x1 * lax.rsqrt(jnp.mean(x1 * x1, axis=-1, keepdims=True) + EPS) * fnw_ref[...]
    half = D_MODEL // 2
    lo = pltpu.bitcast(h[:, :half].astype(BF16).astype(F32), jnp.uint32) >> 16
    hi = pltpu.bitcast(h[:, half:].astype(BF16).astype(F32), jnp.uint32) & jnp.uint32(0xFFFF0000)
    h2_ref[...] = hi | lo
    logits = _dot_hp(h, rw_ref[...]) + rb_ref[...]
    lane = lax.broadcasted_iota(jnp.int32, logits.shape, 1)
    vals = logits
    idx_out = jnp.zeros(logits.shape, jnp.int32)
    member = jnp.zeros(logits.shape, F32)
    top, sel = [], []
    for k in range(TOP_K):
        mk = jnp.max(vals, axis=-1, keepdims=True)
        ik = jnp.min(jnp.where(vals == mk, lane, LANES), axis=-1, keepdims=True)
        top.append(mk)
        sel.append(lane == ik)
        idx_out = jnp.where(lane == k, ik, idx_out)
        member = jnp.where(sel[k], 1.0, member)
        vals = jnp.where(sel[k], -jnp.inf, vals)
    es = [jnp.exp(mk - top[0]) for mk in top]
    inv = 1.0 / (es[0] + es[1] + es[2] + es[3])
    gate_out = jnp.zeros(logits.shape, F32)
    for k in range(TOP_K):
        gate_out = jnp.where(lane == k, es[k] * inv, gate_out)
    idx_ref[...] = idx_out
    gate_ref[...] = gate_out

    tm = logits.shape[0]
    earlier = (lax.broadcasted_iota(jnp.int32, (tm, tm), 0)
               > lax.broadcasted_iota(jnp.int32, (tm, tm), 1)).astype(BF16)
    before = jnp.dot(earlier, member.astype(BF16), preferred_element_type=F32) + cnt_ref[0:1, :]
    rank_out = jnp.zeros(logits.shape, jnp.int32)
    for k in range(TOP_K):
        rk = jnp.sum(jnp.where(sel[k], before, 0.0), axis=-1, keepdims=True)
        rank_out = jnp.where(lane == k, rk.astype(jnp.int32), rank_out)
    rank_ref[...] = rank_out
    cnt_ref[...] = cnt_ref[...] + jnp.sum(member, axis=0, keepdims=True)


def _outproj_both_kernel(odp_ref, ogp_ref, xp_ref, ods_ref, ogs_ref, xs_ref, w1_ref, w2_ref, fnw_ref,
                         rw_ref, rb_ref, *out_refs, npb):
    i = pl.program_id(0)

    @pl.when(i < npb)
    def _():
        _outproj_kernel(odp_ref, ogp_ref, xp_ref, w1_ref, w2_ref, fnw_ref, rw_ref, rb_ref, *out_refs)

    @pl.when(i >= npb)
    def _():
        _outproj_kernel(ods_ref, ogs_ref, xs_ref, w1_ref, w2_ref, fnw_ref, rw_ref, rb_ref, *out_refs)


def _outproj(od_p, og_p, x_p, od_s, og_s, x_s, w1, w2, fnw, rw_pad, rb_pad):
    n_p, n_s = x_p.shape[0], x_s.shape[0]
    m = n_p + n_s
    tm = min(512, n_s)
    npb = n_p // tm
    prow = lambda i: (jnp.minimum(i, npb - 1), 0)
    srow = lambda i: (jnp.maximum(i - npb, 0), 0)
    row = lambda i: (i, 0)
    const = lambda i: (0, 0)
    return pl.pallas_call(
        functools.partial(_outproj_both_kernel, npb=npb),
        out_shape=(jax.ShapeDtypeStruct((m, D_MODEL), F32),
                   jax.ShapeDtypeStruct((m, D_MODEL // 2), jnp.uint32),
                   jax.ShapeDtypeStruct((m, LANES), jnp.int32),
                   jax.ShapeDtypeStruct((m, LANES), F32),
                   jax.ShapeDtypeStruct((m, LANES), jnp.int32),
                   jax.ShapeDtypeStruct((SUBLANES, LANES), F32)),
        grid=(m // tm,),
        in_specs=[pl.BlockSpec((tm, GROUP_W), prow),
                  pl.BlockSpec((tm, GROUP_W), prow),
                  pl.BlockSpec((tm, D_MODEL), prow),
                  pl.BlockSpec((tm, GROUP_W), srow),
                  pl.BlockSpec((tm, GROUP_W), srow),
                  pl.BlockSpec((tm, D_MODEL), srow),
                  pl.BlockSpec((GROUP_W, D_MODEL), const),
                  pl.BlockSpec((GROUP_W, D_MODEL), const),
                  pl.BlockSpec((1, D_MODEL), const),
                  pl.BlockSpec((D_MODEL, LANES), const),
                  pl.BlockSpec((1, LANES), const)],
        out_specs=(pl.BlockSpec((tm, D_MODEL), row),
                   pl.BlockSpec((tm, D_MODEL // 2), row),
                   pl.BlockSpec((tm, LANES), row),
                   pl.BlockSpec((tm, LANES), row),
                   pl.BlockSpec((tm, LANES), row),
                   pl.BlockSpec((SUBLANES, LANES), const)),
        compiler_params=_cparams(("arbitrary",)),
        name="outproj_router",
    )(od_p, og_p, x_p, od_s, og_s, x_s, w1, w2, fnw, rw_pad, rb_pad)


def _cast_kernel(w_ref, o_ref):
    o_ref[...] = w_ref[...].astype(BF16)


def _cast_bf16(w):
    e, r, c = w.shape
    tr = 512
    spec = pl.BlockSpec((1, tr, c), lambda i, j: (i, j, 0))
    return pl.pallas_call(
        _cast_kernel,
        out_shape=jax.ShapeDtypeStruct(w.shape, BF16),
        grid=(e, r // tr),
        in_specs=[spec],
        out_specs=spec,
        compiler_params=_cparams(("arbitrary", "arbitrary")),
        name="cast_bf16",
    )(w)


def _ffn_kernel(be_ref, nu_ref, tok_hbm, h2_hbm, wg_ref, bg_ref, wu_ref, bu_ref, wd_ref, bd_ref, y_ref,
                xa_ref, xb_ref, xbf_ref, ia_ref, ib_ref, isem, rsem, *, nf):
    blk = pl.program_id(0)
    f = pl.program_id(1)
    nu = nu_ref[0]
    slot = blk % 2
    xbufs = (xa_ref, xb_ref)
    idxs = (ia_ref, ib_ref)

    def idx_copy(b, s):
        return pltpu.make_async_copy(tok_hbm.at[b], idxs[s], isem.at[s])

    def issue_rows(s):
        def body(g, carry):
            r0 = pl.multiple_of(g * SUBLANES, SUBLANES)
            for u in range(SUBLANES):
                tok = idxs[s][r0 + u]
                pltpu.make_async_copy(h2_hbm.at[pl.ds(tok, 1)], xbufs[s].at[pl.ds(r0 + u, 1)],
                                      rsem.at[s]).start()
            return carry

        lax.fori_loop(0, EXPERT_BLOCK // SUBLANES, body, 0)

    def wait_rows(s):
        pltpu.make_async_copy(h2_hbm.at[pl.ds(0, EXPERT_BLOCK)], xbufs[s], rsem.at[s]).wait()

    @pl.when(jnp.logical_and(f == 0, blk == 0))
    def _():
        first = idx_copy(0, 0)
        first.start()
        first.wait()
        issue_rows(0)

        @pl.when(1 < nu)
        def _():
            idx_copy(1, 1).start()

    for s in (0, 1):
        @pl.when(jnp.logical_and(jnp.logical_and(f == 0, blk < nu), slot == s))
        def _():
            @pl.when(blk + 1 < nu)
            def _():
                idx_copy(blk + 1, 1 - s).wait()
                issue_rows(1 - s)

            wait_rows(s)
            xp = xbufs[s][...]
            half = D_MODEL // 2
            xbf_ref[:, 0:half] = pltpu.bitcast(xp << 16, F32).astype(BF16)
            xbf_ref[:, half:] = pltpu.bitcast(xp & jnp.uint32(0xFFFF0000), F32).astype(BF16)

        @pl.when(jnp.logical_and(jnp.logical_and(f == nf - 1, blk + 2 < nu), slot == s))
        def _():
            idx_copy(blk + 2, s).start()

    @pl.when(blk < nu)
    def _():
        x = xbf_ref[...]
        gate = jnp.minimum(jnp.dot(x, wg_ref[0], preferred_element_type=F32) + bg_ref[0],
                           SWIGLU_LIMIT)
        up = jnp.clip(jnp.dot(x, wu_ref[0], preferred_element_type=F32) + bu_ref[0],
                      -SWIGLU_LIMIT, SWIGLU_LIMIT)
        act = (up + 1.0) * gate * jax.nn.sigmoid(SWIGLU_ALPHA * gate)
        part = jnp.dot(act.astype(BF16), wd_ref[0], preferred_element_type=F32)

        @pl.when(f == 0)
        def _():
            y_ref[...] = part + bd_ref[0]

        @pl.when(f > 0)
        def _():
            y_ref[...] += part

    @pl.when(jnp.logical_and(blk >= nu, f == 0))
    def _():
        y_ref[...] = jnp.zeros(y_ref.shape, F32)


def _moe_ffn(block_e, n_used, slot_tok, h2, wg, bg, wu, bu, wd, bd):
    n_slots = slot_tok.shape[0]
    nb = n_slots // EXPERT_BLOCK
    nf = D_FF // FFN_TF

    def f_of(b, f, nu):
        return jnp.where(b < nu[0], f, nf - 1)

    grid_spec = pltpu.PrefetchScalarGridSpec(
        num_scalar_prefetch=2,
        grid=(nb, nf),
        in_specs=[pl.BlockSpec(memory_space=pl.ANY),
                  pl.BlockSpec(memory_space=pl.ANY),
                  pl.BlockSpec((1, D_MODEL, FFN_TF), lambda b, f, be, nu: (be[b], 0, f_of(b, f, nu))),
                  pl.BlockSpec((1, 1, FFN_TF), lambda b, f, be, nu: (be[b], 0, f_of(b, f, nu))),
                  pl.BlockSpec((1, D_MODEL, FFN_TF), lambda b, f, be, nu: (be[b], 0, f_of(b, f, nu))),
                  pl.BlockSpec((1, 1, FFN_TF), lambda b, f, be, nu: (be[b], 0, f_of(b, f, nu))),
                  pl.BlockSpec((1, FFN_TF, D_MODEL), lambda b, f, be, nu: (be[b], f_of(b, f, nu), 0)),
                  pl.BlockSpec((1, 1, D_MODEL), lambda b, f, be, nu: (be[b], 0, 0))],
        out_specs=pl.BlockSpec((EXPERT_BLOCK, D_MODEL), lambda b, f, be, nu: (b, 0)),
        scratch_shapes=[pltpu.VMEM((EXPERT_BLOCK, D_MODEL // 2), jnp.uint32),
                        pltpu.VMEM((EXPERT_BLOCK, D_MODEL // 2), jnp.uint32),
                        pltpu.VMEM((EXPERT_BLOCK, D_MODEL), BF16),
                        pltpu.SMEM((EXPERT_BLOCK,), jnp.int32),
                        pltpu.SMEM((EXPERT_BLOCK,), jnp.int32),
                        pltpu.SemaphoreType.DMA((2,)),
                        pltpu.SemaphoreType.DMA((2,))],
    )
    return pl.pallas_call(
        functools.partial(_ffn_kernel, nf=nf),
        out_shape=jax.ShapeDtypeStruct((n_slots, D_MODEL), F32),
        grid_spec=grid_spec,
        compiler_params=_cparams(("arbitrary", "arbitrary")),
        name="moe_ffn",
    )(block_e, n_used, slot_tok.reshape(nb, EXPERT_BLOCK), h2, wg, bg, wu, bu, wd, bd)


def _combine_kernel(pos_hbm, ys_hbm, x1_ref, gate_ref, fnw_ref, y_ref, pa_ref, pb_ref, ya_ref, yb_ref,
                    isem, rsem, *, tm, nb):
    blk = pl.program_id(0)
    slot = blk % 2
    poss = (pa_ref, pb_ref)
    ybufs = (ya_ref, yb_ref)

    def idx_copy(b, s):
        return pltpu.make_async_copy(pos_hbm.at[b], poss[s], isem.at[s])

    def issue_rows(s):
        def body(g, carry):
            t0 = pl.multiple_of(g * SUBLANES, SUBLANES)
            for u in range(SUBLANES):
                for k in range(TOP_K):
                    p = poss[s][(t0 + u) * TOP_K + k]
                    pltpu.make_async_copy(ys_hbm.at[pl.ds(p, 1)], ybufs[s].at[k, pl.ds(t0 + u, 1)],
                                          rsem.at[s]).start()
            return carry

        lax.fori_loop(0, tm // SUBLANES, body, 0)

    def wait_rows(s):
        for k in range(TOP_K):
            pltpu.make_async_copy(ys_hbm.at[pl.ds(0, tm)], ybufs[s].at[k], rsem.at[s]).wait()

    @pl.when(blk == 0)
    def _():
        first = idx_copy(0, 0)
        first.start()
        first.wait()
        issue_rows(0)
        if nb > 1:
            idx_copy(1, 1).start()

    for s in (0, 1):
        @pl.when(jnp.logical_and(blk + 1 < nb, slot == s))
        def _():
            idx_copy(blk + 1, 1 - s).wait()
            issue_rows(1 - s)

        @pl.when(jnp.logical_and(blk + 2 < nb, slot == s))
        def _():
            idx_copy(blk + 2, s).start()

        @pl.when(slot == s)
        def _():
            wait_rows(s)
            g = gate_ref[...]
            moe = g[:, 0:1] * ybufs[s][0]
            for k in range(1, TOP_K):
                moe = moe + g[:, k:k + 1] * ybufs[s][k]
            x2 = x1_ref[...] + moe
            y_ref[...] = (x2 * lax.rsqrt(jnp.mean(x2 * x2, axis=-1, keepdims=True) + EPS)
                          * fnw_ref[...])


def _combine(pos, y_slots, x1, gates, final_norm_w, row0, m):
    tm = min(COMBINE_TM, m)
    nb = m // tm
    off = row0 // tm
    row = lambda i: (i + off, 0)
    return pl.pallas_call(
        functools.partial(_combine_kernel, tm=tm, nb=nb),
        out_shape=jax.ShapeDtypeStruct((m, D_MODEL), F32),
        grid=(nb,),
        in_specs=[pl.BlockSpec(memory_space=pl.ANY),
                  pl.BlockSpec(memory_space=pl.ANY),
                  pl.BlockSpec((tm, D_MODEL), row),
                  pl.BlockSpec((tm, LANES), row),
                  pl.BlockSpec((1, D_MODEL), lambda i: (0, 0))],
        out_specs=pl.BlockSpec((tm, D_MODEL), lambda i: (i, 0)),
        scratch_shapes=[pltpu.SMEM((tm * TOP_K,), jnp.int32),
                        pltpu.SMEM((tm * TOP_K,), jnp.int32),
                        pltpu.VMEM((TOP_K, tm, D_MODEL), F32),
                        pltpu.VMEM((TOP_K, tm, D_MODEL), F32),
                        pltpu.SemaphoreType.DMA((2,)),
                        pltpu.SemaphoreType.DMA((2,))],
        compiler_params=_cparams(("arbitrary",)),
        name="moe_combine",
    )(pos.reshape(nb, tm * TOP_K), y_slots, x1, gates, final_norm_w.reshape(1, D_MODEL))


def _route(streams):
    experts = jnp.arange(N_EXPERTS, dtype=jnp.int32)
    counts = sum(c for _, _, c in streams)
    padded = (counts + EXPERT_BLOCK - 1) // EXPERT_BLOCK * EXPERT_BLOCK
    padded_end = jnp.cumsum(padded)
    base = padded_end - padded
    dests = []
    for top_idx, rank, c in streams:
        onehot = (top_idx[:, :, None] == experts[None, None, :]).astype(jnp.int32)
        dests.append((jnp.sum(onehot * base[None, None, :], axis=-1) + rank).reshape(-1).astype(jnp.int32))
        base = base + c
    n_assign = sum(d.shape[0] for d in dests)
    n_blocks = -(-n_assign // EXPERT_BLOCK) + N_EXPERTS
    flat_tok = jnp.arange(n_assign, dtype=jnp.int32) // TOP_K
    slot_tok = jnp.zeros((n_blocks * EXPERT_BLOCK,), jnp.int32).at[jnp.concatenate(dests)].set(
        flat_tok, mode="promise_in_bounds", unique_indices=True)
    starts = jnp.arange(n_blocks, dtype=jnp.int32) * EXPERT_BLOCK
    block_e = jnp.minimum(jnp.sum((padded_end[None, :] <= starts[:, None]).astype(jnp.int32), axis=1),
                          N_EXPERTS - 1).astype(jnp.int32)
    n_used = (padded_end[-1] // EXPERT_BLOCK).astype(jnp.int32).reshape(1)
    return slot_tok, dests, block_e, n_used


def _mixer(x, pos, k_cache, v_cache, conv_past, s0, lam, attn_norm_w, w_perm, conv_w_t, neg_a, dt_b,
           diff_norm_w, gdn_norm_w):
    b, t, _ = x.shape
    m = b * t
    tm = min(1024, m)
    tabs = _rope_tables(pos)
    if t < tm:
        tabs = tuple(jnp.tile(a, (tm // t, 1)) for a in tabs)
    k_out, v_out, rest, qkv = _inproj(x.reshape(m, D_MODEL), attn_norm_w, w_perm, tabs)
    proj3 = rest.reshape(b, t, REST_COLS)
    qkv3 = qkv.reshape(b, t, 3 * GROUP_W)
    k_rows = k_out.reshape(b, t, N_HEADS, HEAD)
    v_rows = v_out.reshape(b, t, N_HEADS, HEAD)

    if k_cache is None:
        tk = min(512, t // 4)
        vt = qkv3[:, :, 2 * GROUP_W:].reshape(b, t // tk, tk, N_HEADS, HEAD).transpose(0, 3, 1, 4, 2)
        o_d = _diff_attn_prompt(lam, qkv3, vt, diff_norm_w, tk=tk)
    else:
        past = k_cache.shape[1]
        k_all = jnp.concatenate([k_cache.reshape(b, past, GROUP_W).astype(BF16),
                                 qkv3[:, :, GROUP_W:2 * GROUP_W]], axis=1)
        v_all = jnp.concatenate([v_cache.reshape(b, past, N_HEADS, HEAD).astype(BF16),
                                 qkv3[:, :, 2 * GROUP_W:].reshape(b, t, N_HEADS, HEAD)], axis=1)
        vt = v_all.transpose(0, 2, 3, 1)[:, :, None]
        o_d = _diff_attn_full(lam, qkv3, k_all, vt, diff_norm_w)

    conv_past8 = jnp.pad(conv_past, ((0, 0), (SUBLANES - (CONV_K - 1), 0), (0, 0)))
    o_g, s_new, conv8 = _gdn(proj3, conv_past8, s0, conv_w_t, neg_a, dt_b, gdn_norm_w)
    conv_new = conv8[:, SUBLANES - (CONV_K - 1):, :]
    return o_d.reshape(m, GROUP_W), o_g.reshape(m, GROUP_W), k_rows, v_rows, conv_new, s_new


def kernel(x_prompt, x_sample, cache_k_diff, cache_v_diff, state_conv, state_gdn, attn_norm_w, w_in,
           conv_w, a_log, dt_bias, lambda_q1, lambda_k1, lambda_q2, lambda_k2, diff_norm_w, gdn_norm_w,
           w_out, ffn_norm_w, router_w, router_b, w_gate, b_gate, w_up, b_up, w_down, b_down,
           final_norm_w):
    l = 0
    bp, tp, _ = x_prompt.shape
    bs, ts, _ = x_sample.shape
    past_len = cache_k_diff.shape[2]

    lam = (jnp.exp(jnp.sum(lambda_q1[l] * lambda_k1[l])) - jnp.exp(jnp.sum(lambda_q2[l] * lambda_k2[l]))
           + LAM_INIT).astype(F32).reshape(1, 1)

    wl = w_in[l]
    ncol = wl.shape[1]
    w_perm = jnp.concatenate(
        [wl[:, :COL_Z], wl[:, ncol - GROUP_W:], wl[:, COL_Z:COL_Z + 2 * N_HEADS],
         jnp.zeros((D_MODEL, PROJ_COLS - ncol), F32)], axis=1).astype(BF16)
    conv_w_t = conv_w[l].T
    pad8 = lambda a: jnp.pad(a.astype(F32), (0, LANES - N_HEADS)).reshape(1, LANES)
    neg_a = pad8(-jnp.exp(a_log[l]))
    dt_b = pad8(dt_bias[l])
    w1 = w_out[l][:GROUP_W].astype(BF16)
    w2 = w_out[l][GROUP_W:].astype(BF16)
    rw_pad = jnp.pad(router_w[l], ((0, 0), (0, LANES - N_EXPERTS)))
    rb_pad = jnp.concatenate([router_b[l].astype(F32),
                              jnp.full((LANES - N_EXPERTS,), -jnp.inf, F32)]).reshape(1, LANES)
    wg = _cast_bf16(w_gate[l])
    wu = _cast_bf16(w_up[l])
    wd = _cast_bf16(w_down[l])
    bg = b_gate[l].reshape(N_EXPERTS, 1, D_FF)
    bu = b_up[l].reshape(N_EXPERTS, 1, D_FF)
    bd = b_down[l].reshape(N_EXPERTS, 1, D_MODEL)

    mix_args = (lam, attn_norm_w[l], w_perm, conv_w_t, neg_a, dt_b, diff_norm_w[l], gdn_norm_w[l])
    conv0 = jnp.zeros((bp, CONV_K - 1, 3 * GROUP_W), F32)
    s0 = jnp.zeros((bp, N_HEADS, HEAD, HEAD), F32)
    od_p, og_p, kp, vp, cp, sp = _mixer(x_prompt, jnp.arange(tp), None, None, conv0, s0, *mix_args)
    od_s, og_s, ks, vs, cs, ss = _mixer(x_sample, past_len + jnp.arange(ts), cache_k_diff[l],
                                        cache_v_diff[l], state_conv[l], state_gdn[l], *mix_args)

    fnw = ffn_norm_w[l].reshape(1, D_MODEL)
    n_p, n_s = bp * tp, bs * ts
    x1, h2, idx, gate, rank, cnt = _outproj(
        od_p, og_p, x_prompt.reshape(n_p, D_MODEL), od_s, og_s, x_sample.reshape(n_s, D_MODEL),
        w1, w2, fnw, rw_pad, rb_pad)
    slot_tok, (dest,), block_e, n_used = _route(
        [(idx[:, :TOP_K], rank[:, :TOP_K], cnt[0, :N_EXPERTS].astype(jnp.int32))])
    y_slots = _moe_ffn(block_e, n_used, slot_tok, h2, wg, bg, wu, bu, wd, bd)
    y_p = _combine(dest[:n_p * TOP_K], y_slots, x1, gate, final_norm_w, 0, n_p)
    y_s = _combine(dest[n_p * TOP_K:], y_slots, x1, gate, final_norm_w, n_p, n_s)

    return (y_p.reshape(bp, tp, D_MODEL), y_s.reshape(bs, ts, D_MODEL),
            kp[None], vp[None], cp[None], sp[None], ks[None], vs[None], cs[None], ss[None])
```

```python
import functools
import math

import jax
import jax.numpy as jnp
from jax import lax
from jax.experimental import pallas as pl
from jax.experimental.pallas import tpu as pltpu

F32 = jnp.float32
BF16 = jnp.bfloat16

D_MODEL = 2048
N_HEADS = 8
HEAD = 128
QK_DIM = 64
ROT_DIM = 16
ROT_HALF = ROT_DIM // 2
ROPE_THETA = 500000.0
CHUNK = 64
GROUP_W = N_HEADS * HEAD
CONV_K = 4
N_EXPERTS = 32
TOP_K = 4
D_FF = 2048
SWIGLU_LIMIT = 7.0
SWIGLU_ALPHA = 1.702
EPS = 1e-6
SUBLN_EPS = 1e-5
LAM_INIT = 0.8 - 0.6 * math.exp(-0.3 * 0)
Q_SCALE = QK_DIM ** -0.5 * math.log2(math.e)

LANES = 128
SUBLANES = 8
VMEM_LIMIT = 56 * 1024 * 1024

PROJ_TN = 512
PROJ_COLS = 7680
COL_Z = 6 * GROUP_W
REST_COLS = PROJ_COLS - 3 * GROUP_W
REST_Z = 3 * GROUP_W
REST_AB = 4 * GROUP_W

EXPERT_BLOCK = 512
FFN_TF = 1024
COMBINE_TM = 256
GDN_BATCH = 2


def _cparams(sem, vmem=VMEM_LIMIT):
    return pltpu.CompilerParams(dimension_semantics=sem, vmem_limit_bytes=vmem)


def _inproj_kernel(x_ref, nw_ref, w_ref, c_ref, s1_ref, s2_ref, k_ref, v_ref, rest_ref, qkv_ref, h_ref):
    j = pl.program_id(1)

    @pl.when(j == 0)
    def _():
        x = x_ref[...]
        ms = jnp.mean(x * x, axis=-1, keepdims=True)
        h_ref[...] = (x * lax.rsqrt(ms + EPS) * nw_ref[...]).astype(BF16)

    acc = jnp.dot(h_ref[...], w_ref[...], preferred_element_type=F32)

    def rope(g):
        xg = acc[:, g * LANES:(g + 1) * LANES]
        return (xg * c_ref[...] + pltpu.roll(xg, LANES - ROT_HALF, 1) * s1_ref[...]
                + pltpu.roll(xg, ROT_HALF, 1) * s2_ref[...])

    @pl.when(j < 2)
    def _():
        for g in range(PROJ_TN // LANES):
            qkv_ref[:, g * LANES:(g + 1) * LANES] = (rope(g) * Q_SCALE).astype(BF16)

    @pl.when(jnp.logical_and(j >= 2, j < 4))
    def _():
        for g in range(PROJ_TN // LANES):
            r = rope(g)
            k_ref[:, g * LANES:(g + 1) * LANES] = r
            qkv_ref[:, g * LANES:(g + 1) * LANES] = r.astype(BF16)

    @pl.when(jnp.logical_and(j >= 4, j < 6))
    def _():
        v_ref[...] = acc
        qkv_ref[...] = acc.astype(BF16)

    @pl.when(j >= 6)
    def _():
        rest_ref[...] = acc


def _rope_tables(pos):
    inv_freq = jnp.power(ROPE_THETA, -jnp.arange(ROT_HALF, dtype=F32) * (2.0 / ROT_DIM))
    ang = pos.astype(F32)[:, None] * inv_freq[None, :]
    cos, sin = jnp.cos(ang), jnp.sin(ang)
    t = pos.shape[0]
    one = jnp.ones((t, QK_DIM - ROT_DIM), F32)
    zero = jnp.zeros((t, QK_DIM - ROT_DIM), F32)
    z8 = jnp.zeros((t, ROT_HALF), F32)
    c = jnp.concatenate([cos, cos, one], axis=1)
    s1 = jnp.concatenate([-sin, z8, zero], axis=1)
    s2 = jnp.concatenate([z8, sin, zero], axis=1)
    return tuple(jnp.concatenate([a, a], axis=1) for a in (c, s1, s2))


def _inproj(x2d, norm_w, w_perm, tables):
    m = x2d.shape[0]
    tm = min(1024, m)
    tab_rows = tables[0].shape[0]
    n_tab = tab_rows // tm
    grid = (m // tm, PROJ_COLS // PROJ_TN)
    tab_spec = pl.BlockSpec((tm, LANES), lambda i, j: (i % n_tab, 0))
    return pl.pallas_call(
        _inproj_kernel,
        out_shape=(jax.ShapeDtypeStruct((m, GROUP_W), F32),
                   jax.ShapeDtypeStruct((m, GROUP_W), F32),
                   jax.ShapeDtypeStruct((m, REST_COLS), F32),
                   jax.ShapeDtypeStruct((m, 3 * GROUP_W), BF16)),
        grid=grid,
        in_specs=[pl.BlockSpec((tm, D_MODEL), lambda i, j: (i, 0)),
                  pl.BlockSpec((1, D_MODEL), lambda i, j: (0, 0)),
                  pl.BlockSpec((D_MODEL, PROJ_TN), lambda i, j: (0, j)),
                  tab_spec, tab_spec, tab_spec],
        out_specs=(pl.BlockSpec((tm, PROJ_TN), lambda i, j: (i, jnp.clip(j - 2, 0, 1))),
                   pl.BlockSpec((tm, PROJ_TN), lambda i, j: (i, jnp.clip(j - 4, 0, 1))),
                   pl.BlockSpec((tm, PROJ_TN), lambda i, j: (i, jnp.maximum(j - 6, 0))),
                   pl.BlockSpec((tm, PROJ_TN), lambda i, j: (i, jnp.minimum(j, 5)))),
        scratch_shapes=[pltpu.VMEM((tm, D_MODEL), BF16)],
        compiler_params=_cparams(("arbitrary", "arbitrary")),
        name="inproj",
    )(x2d, norm_w.reshape(1, D_MODEL), w_perm, *tables)


def _split_maps(q_ref, qq_ref, tq):
    q = q_ref[0]
    lane = lax.broadcasted_iota(jnp.int32, (tq, LANES), 1)
    zero = jnp.zeros_like(q)
    qq_ref[0:tq, :] = jnp.where(lane < QK_DIM, q, zero)
    qq_ref[tq:2 * tq, :] = jnp.where(lane >= QK_DIM, q, zero)


def _attn_finish(lam_ref, nw_ref, o_ref, l_ref, acc_ref, tq):
    lam = lam_ref[0, 0]
    inv_l = 1.0 / l_ref[...]
    o = acc_ref[...] * inv_l
    o = o[:, 0:tq] - lam * o[:, tq:2 * tq]
    ms = jnp.mean(o * o, axis=0, keepdims=True)
    on = (o * lax.rsqrt(ms + SUBLN_EPS)).T
    o_ref[0] = ((on * nw_ref[...]) * (1.0 - LAM_INIT)).astype(BF16)


def _attn_full_kernel(lam_ref, q_ref, k_ref, vt_ref, nw_ref, o_ref, qq_ref, l_ref, acc_ref, *, tq):
    _split_maps(q_ref, qq_ref, tq)
    s = lax.dot_general(k_ref[0], qq_ref[...], (((1,), (1,)), ((), ())),
                        preferred_element_type=F32)
    p = jnp.exp2(s - jnp.max(s, axis=0, keepdims=True))
    l_ref[...] = jnp.sum(p, axis=0, keepdims=True)
    acc_ref[...] = jnp.dot(vt_ref[0, 0, 0], p.astype(BF16), preferred_element_type=F32)
    _attn_finish(lam_ref, nw_ref, o_ref, l_ref, acc_ref, tq)


def _diff_attn_full(lam, q_src, k_all, vt, norm_w):
    b, tq = q_src.shape[0], q_src.shape[1]
    t_k = k_all.shape[1]
    return pl.pallas_call(
        functools.partial(_attn_full_kernel, tq=tq),
        out_shape=jax.ShapeDtypeStruct((b, tq, GROUP_W), BF16),
        grid=(b, N_HEADS),
        in_specs=[pl.BlockSpec(memory_space=pltpu.SMEM),
                  pl.BlockSpec((1, tq, HEAD), lambda bi, h: (bi, 0, h)),
                  pl.BlockSpec((1, t_k, HEAD), lambda bi, h: (bi, 0, h)),
                  pl.BlockSpec((1, 1, 1, HEAD, t_k), lambda bi, h: (bi, h, 0, 0, 0)),
                  pl.BlockSpec((1, HEAD), lambda bi, h: (0, 0))],
        out_specs=pl.BlockSpec((1, tq, HEAD), lambda bi, h: (bi, 0, h)),
        scratch_shapes=[pltpu.VMEM((2 * tq, HEAD), BF16),
                        pltpu.VMEM((1, 2 * tq), F32),
                        pltpu.VMEM((HEAD, 2 * tq), F32)],
        compiler_params=_cparams(("arbitrary", "arbitrary")),
        name="diff_attn_full",
    )(lam, q_src, k_all, vt, norm_w.reshape(1, HEAD))


def _attn_pipe_kernel(lam_ref, q_ref, k_ref, vt_ref, nw_ref, o_ref,
                      qq_ref, sa_ref, sb_ref, pa_ref, pb_ref, ala_ref, alb_ref, ma_ref, mb_ref, m_ref, l_ref,
                      acc_ref, *, tq, tk):
    i = pl.program_id(2)
    _split_maps(q_ref, qq_ref, tq)
    m_ref[...] = jnp.full(m_ref.shape, -jnp.inf, F32)
    l_ref[...] = jnp.zeros(l_ref.shape, F32)
    acc_ref[...] = jnp.zeros(acc_ref.shape, F32)
    s_bufs = (sa_ref, sb_ref)
    p_bufs = (pa_ref, pb_ref)
    al_bufs = (ala_ref, alb_ref)
    mx_bufs = (ma_ref, mb_ref)

    def scores(j, par, masked):
        kb = k_ref[0, pl.ds(pl.multiple_of(j * tk, tk), tk), :]
        s = lax.dot_general(kb, qq_ref[...], (((1,), (1,)), ((), ())), preferred_element_type=F32)
        if masked:
            kchunk = (j * tk + lax.broadcasted_iota(jnp.int32, (tk, 2 * tq), 0)) // CHUNK
            qcol = lax.broadcasted_iota(jnp.int32, (tk, 2 * tq), 1) & (tq - 1)
            s = jnp.where(kchunk <= (i * tq + qcol) // CHUNK, s, -jnp.inf)
        s_bufs[par][...] = s
        mx_bufs[par][...] = jnp.max(s, axis=0, keepdims=True)

    def softmax(par):
        s = s_bufs[par][...]
        m_old = m_ref[...]
        m_new = jnp.maximum(m_old, mx_bufs[par][...])
        alpha = jnp.exp2(m_old - m_new)
        p = jnp.exp2(s - m_new)
        l_ref[...] = alpha * l_ref[...] + jnp.sum(p, axis=0, keepdims=True)
        m_ref[...] = m_new
        p_bufs[par][...] = p.astype(BF16)
        al_bufs[par][...] = alpha

    def pv(j, par):
        upd = jnp.dot(vt_ref[0, 0, j], p_bufs[par][...], preferred_element_type=F32)
        acc_ref[...] = acc_ref[...] * al_bufs[par][...] + upd

    n = 2 * i + 2
    scores(0, 0, True)
    scores(1, 1, True)
    softmax(0)

    def pair(u, carry):
        t = 2 * u + 1
        scores(t + 1, 0, False)
        softmax(1)
        pv(t - 1, 0)
        scores(t + 2, 1, False)
        softmax(0)
        pv(t, 1)
        return carry

    lax.fori_loop(0, i - 1, pair, 0)

    @pl.when(i > 0)
    def _():
        t = n - 3
        scores(t + 1, 0, True)
        softmax(1)
        pv(t - 1, 0)
        scores(t + 2, 1, True)
        softmax(0)
        pv(t, 1)

    softmax(1)
    pv(n - 2, 0)
    pv(n - 1, 1)
    _attn_finish(lam_ref, nw_ref, o_ref, l_ref, acc_ref, tq)


def _diff_attn_prompt(lam, qkv3, vt, norm_w, *, tk):
    b, t = qkv3.shape[0], qkv3.shape[1]
    tq = 2 * tk
    nkb = t // tk
    kern = functools.partial(_attn_pipe_kernel, tq=tq, tk=tk)
    stat = pltpu.VMEM((1, 2 * tq), F32)
    return pl.pallas_call(
        kern,
        out_shape=jax.ShapeDtypeStruct((b, t, GROUP_W), BF16),
        grid=(b, N_HEADS, t // tq),
        in_specs=[pl.BlockSpec(memory_space=pltpu.SMEM),
                  pl.BlockSpec((1, tq, HEAD), lambda bi, h, i: (bi, i, h)),
                  pl.BlockSpec((1, t, HEAD), lambda bi, h, i: (bi, 0, N_HEADS + h)),
                  pl.BlockSpec((1, 1, nkb, HEAD, tk), lambda bi, h, i: (bi, h, 0, 0, 0)),
                  pl.BlockSpec((1, HEAD), lambda bi, h, i: (0, 0))],
        out_specs=pl.BlockSpec((1, tq, HEAD), lambda bi, h, i: (bi, i, h)),
        scratch_shapes=[pltpu.VMEM((2 * tq, HEAD), BF16),
                        pltpu.VMEM((tk, 2 * tq), F32),
                        pltpu.VMEM((tk, 2 * tq), F32),
                        pltpu.VMEM((tk, 2 * tq), BF16),
                        pltpu.VMEM((tk, 2 * tq), BF16),
                        stat, stat, stat, stat, stat, stat,
                        pltpu.VMEM((HEAD, 2 * tq), F32)],
        compiler_params=_cparams(("arbitrary", "arbitrary", "arbitrary")),
        name="diff_attn_prompt",
    )(lam, qkv3, qkv3, vt, norm_w.reshape(1, HEAD))


def _bdot(a, b):
    return jnp.dot(a.astype(BF16), b.astype(BF16), preferred_element_type=F32)


def _bdot_nt(a, b):
    return lax.dot_general(a.astype(BF16), b.astype(BF16), (((1,), (1,)), ((), ())),
                           preferred_element_type=F32)


def _bdot_tn(a, b):
    return lax.dot_general(a.astype(BF16), b.astype(BF16), (((0,), (0,)), ((), ())),
                           preferred_element_type=F32)


def _split3(a):
    hi = a.astype(BF16)
    r = a - hi.astype(F32)
    mid = r.astype(BF16)
    lo = (r - mid.astype(F32)).astype(BF16)
    return hi, mid, lo


def _dot_hp(a, b):
    a_hi = a.astype(BF16)
    a_lo = (a - a_hi.astype(F32)).astype(BF16)
    b_hi = b.astype(BF16)
    b_lo = (b - b_hi.astype(F32)).astype(BF16)
    d = functools.partial(jnp.dot, preferred_element_type=F32)
    return d(a_hi, b_hi) + (d(a_hi, b_lo) + d(a_lo, b_hi))


def _gdn_kernel(qkv_ref, z_ref, ab_ref, convp_ref, s0_ref, cw_ref, alog_ref, dtb_ref, gnw_ref,
                og_ref, sfin_ref, convn_ref, xbuf_ref, s_ref, *, nc, nb):
    c = pl.program_id(1)

    @pl.when(c == 0)
    def _():
        xbuf_ref[:, 0:SUBLANES, :] = convp_ref[...]
        s_ref[...] = s0_ref[...]

    xbuf_ref[:, SUBLANES:SUBLANES + CHUNK, :] = qkv_ref[...]

    row = lax.broadcasted_iota(jnp.int32, (CHUNK, CHUNK), 0)
    col = lax.broadcasted_iota(jnp.int32, (CHUNK, CHUNK), 1)
    incl = row >= col
    strict = row > col
    blk = (row // 16) == (col // 16)
    eye = (row == col).astype(F32)
    tri = incl.astype(BF16)
    d = functools.partial(jnp.dot, preferred_element_type=F32)
    S = [(bb, h) for bb in range(nb) for h in range(N_HEADS)]

    gc_all, gc_t, beta_all = {}, {}, {}
    for bb in range(nb):
        ab = ab_ref[bb]
        g_all = alog_ref[...] * jax.nn.softplus(ab + dtb_ref[...])
        g_hi, g_mid, g_lo = _split3(g_all)
        gc_all[bb] = d(tri, g_hi) + (d(tri, g_mid) + d(tri, g_lo))
        gc_t[bb] = gc_all[bb].T
        beta_all[bb] = jax.nn.sigmoid(ab)

    def conv_silu(bb, col0):
        cs = slice(col0, col0 + HEAD)
        y = xbuf_ref[bb, 5:5 + CHUNK, cs] * cw_ref[0:1, cs]
        y = y + xbuf_ref[bb, 6:6 + CHUNK, cs] * cw_ref[1:2, cs]
        y = y + xbuf_ref[bb, 7:7 + CHUNK, cs] * cw_ref[2:3, cs]
        y = y + xbuf_ref[bb, 8:8 + CHUNK, cs] * cw_ref[3:4, cs]
        return y * jax.nn.sigmoid(y)

    qr = {k: conv_silu(k[0], k[1] * HEAD) for k in S}
    kr = {k: conv_silu(k[0], GROUP_W + k[1] * HEAD) for k in S}
    v = {k: conv_silu(k[0], 2 * GROUP_W + k[1] * HEAD) for k in S}
    qn = {k: qr[k] * lax.rsqrt(jnp.sum(qr[k] * qr[k], axis=-1, keepdims=True) + 1e-6) * (HEAD ** -0.5)
          for k in S}
    kn = {k: kr[k] * lax.rsqrt(jnp.sum(kr[k] * kr[k], axis=-1, keepdims=True) + 1e-6) for k in S}
    beta = {k: beta_all[k[0]][:, SUBLANES + k[1]:SUBLANES + k[1] + 1] for k in S}
    gc_col = {k: gc_all[k[0]][:, k[1]:k[1] + 1] for k in S}
    g_last = {k: gc_all[k[0]][CHUNK - 1:CHUNK, k[1]:k[1] + 1] for k in S}
    decay = {k: jnp.where(incl, jnp.exp(jnp.where(incl, gc_col[k] - gc_t[k[0]][k[1]:k[1] + 1, :], 0.0)), 0.0)
             for k in S}
    kb = {k: kn[k] * beta[k] for k in S}
    a = {k: jnp.where(strict, _bdot_nt(kb[k], kn[k]) * decay[k], 0.0) for k in S}
    qk = {k: jnp.where(incl, _bdot_nt(qn[k], kn[k]) * decay[k], 0.0) for k in S}
    eg = {k: jnp.exp(gc_col[k]) for k in S}
    rhs = {k: jnp.concatenate([v[k] * beta[k], kb[k] * eg[k]], axis=1) for k in S}

    ad = {k: jnp.where(blk, a[k], 0.0) for k in S}
    ao = {k: jnp.where(blk, 0.0, a[k]) for k in S}
    a2 = {k: _bdot(ad[k], ad[k]) for k in S}
    a4 = {k: _bdot(a2[k], a2[k]) for k in S}
    a8 = {k: _bdot(a4[k], a4[k]) for k in S}
    p = {k: eye - ad[k] for k in S}
    p = {k: p[k] + _bdot(p[k], a2[k]) for k in S}
    p = {k: p[k] + _bdot(p[k], a4[k]) for k in S}
    p = {k: p[k] + _bdot(p[k], a8[k]) for k in S}
    y = {k: _bdot(p[k], rhs[k]) for k in S}
    bm = {k: _bdot(p[k], ao[k]) for k in S}
    b2 = {k: _bdot(bm[k], bm[k]) for k in S}
    zz = {k: y[k] + _bdot(b2[k], y[k]) for k in S}
    sol = {k: zz[k] - _bdot(bm[k], zz[k]) for k in S}

    s_old = {k: s_ref[k[0], k[1]] for k in S}
    v_new = {k: sol[k][:, 0:HEAD] - _bdot(sol[k][:, HEAD:2 * HEAD], s_old[k]) for k in S}
    o = {k: _bdot(qn[k] * eg[k], s_old[k]) + _bdot(qk[k], v_new[k]) for k in S}
    k_dec = {k: kn[k] * jnp.exp(g_last[k] - gc_col[k]) for k in S}
    for k in S:
        s_ref[k[0], k[1]] = s_old[k] * jnp.exp(g_last[k]) + _bdot_tn(k_dec[k], v_new[k])
    for k in S:
        bb, h = k
        on = o[k] * lax.rsqrt(jnp.mean(o[k] * o[k], axis=-1, keepdims=True) + EPS) * gnw_ref[...]
        zh = z_ref[bb, :, h * HEAD:(h + 1) * HEAD]
        og_ref[bb, :, h * HEAD:(h + 1) * HEAD] = (on * (zh * jax.nn.sigmoid(zh))).astype(BF16)

    xbuf_ref[:, 0:SUBLANES, :] = xbuf_ref[:, CHUNK:CHUNK + SUBLANES, :]

    @pl.when(c == nc - 1)
    def _():
        sfin_ref[...] = s_ref[...]
        convn_ref[...] = xbuf_ref[:, CHUNK:CHUNK + SUBLANES, :]


def _gdn(proj3, conv_past8, s0, conv_w_t, neg_a, dt_bias, gnw):
    b, t = proj3.shape[0], proj3.shape[1]
    nc = t // CHUNK
    nb = GDN_BATCH if b % GDN_BATCH == 0 else 1
    w3 = 3 * GROUP_W
    return pl.pallas_call(
        functools.partial(_gdn_kernel, nc=nc, nb=nb),
        out_shape=(jax.ShapeDtypeStruct((b, t, GROUP_W), BF16),
                   jax.ShapeDtypeStruct((b, N_HEADS, HEAD, HEAD), F32),
                   jax.ShapeDtypeStruct((b, SUBLANES, w3), F32)),
        grid=(b // nb, nc),
        in_specs=[pl.BlockSpec((nb, CHUNK, w3), lambda bi, c: (bi, c, 0)),
                  pl.BlockSpec((nb, CHUNK, GROUP_W), lambda bi, c: (bi, c, REST_Z // GROUP_W)),
                  pl.BlockSpec((nb, CHUNK, LANES), lambda bi, c: (bi, c, REST_AB // LANES)),
                  pl.BlockSpec((nb, SUBLANES, w3), lambda bi, c: (bi, 0, 0)),
                  pl.BlockSpec((nb, N_HEADS, HEAD, HEAD), lambda bi, c: (bi, 0, 0, 0)),
                  pl.BlockSpec((CONV_K, w3), lambda bi, c: (0, 0)),
                  pl.BlockSpec((1, LANES), lambda bi, c: (0, 0)),
                  pl.BlockSpec((1, LANES), lambda bi, c: (0, 0)),
                  pl.BlockSpec((1, HEAD), lambda bi, c: (0, 0))],
        out_specs=(pl.BlockSpec((nb, CHUNK, GROUP_W), lambda bi, c: (bi, c, 0)),
                   pl.BlockSpec((nb, N_HEADS, HEAD, HEAD), lambda bi, c: (bi, 0, 0, 0)),
                   pl.BlockSpec((nb, SUBLANES, w3), lambda bi, c: (bi, 0, 0))),
        scratch_shapes=[pltpu.VMEM((nb, CHUNK + SUBLANES, w3), F32),
                        pltpu.VMEM((nb, N_HEADS, HEAD, HEAD), F32)],
        compiler_params=_cparams(("arbitrary", "arbitrary")),
        name="gdn",
    )(proj3, proj3, proj3, conv_past8, s0, conv_w_t, neg_a, dt_bias, gnw.reshape(1, HEAD))


def _outproj_kernel(od_ref, og_ref, x_ref, w1_ref, w2_ref, fnw_ref, rw_ref, rb_ref,
                    x1_ref, h2_ref, idx_ref, gate_ref, rank_ref, cnt_ref):
    @pl.when(pl.program_id(0) == 0)
    def _():
        cnt_ref[...] = jnp.zeros(cnt_ref.shape, F32)

    acc = jnp.dot(od_ref[...], w1_ref[...], preferred_element_type=F32)
    acc = acc + jnp.dot(og_ref[...], w2_ref[...], preferred_element_type=F32)
    x1 = x_ref[...] + acc
    x1_ref[...] = x1
    h = x1 * lax.rsqrt(jnp.mean(x1 * x1, axis=-1, keepdims=True) + EPS) * fnw_ref[...]
    half = D_MODEL // 2
    lo = pltpu.bitcast(h[:, :half].astype(BF16).astype(F32), jnp.uint32) >> 16
    hi = pltpu.bitcast(h[:, half:].astype(BF16).astype(F32), jnp.uint32) & jnp.uint32(0xFFFF0000)
    h2_ref[...] = hi | lo
    logits = _dot_hp(h, rw_ref[...]) + rb_ref[...]
    lane = lax.broadcasted_iota(jnp.int32, logits.shape, 1)
    vals = logits
    idx_out = jnp.zeros(logits.shape, jnp.int32)
    member = jnp.zeros(logits.shape, F32)
    top, sel = [], []
    for k in range(TOP_K):
        mk = jnp.max(vals, axis=-1, keepdims=True)
        ik = jnp.min(jnp.where(vals == mk, lane, LANES), axis=-1, keepdims=True)
        top.append(mk)
        sel.append(lane == ik)
        idx_out = jnp.where(lane == k, ik, idx_out)
        member = jnp.where(sel[k], 1.0, member)
        vals = jnp.where(sel[k], -jnp.inf, vals)
    es = [jnp.exp(mk - top[0]) for mk in top]
    inv = 1.0 / (es[0] + es[1] + es[2] + es[3])
    gate_out = jnp.zeros(logits.shape, F32)
    for k in range(TOP_K):
        gate_out = jnp.where(lane == k, es[k] * inv, gate_out)
    idx_ref[...] = idx_out
    gate_ref[...] = gate_out

    tm = logits.shape[0]
    earlier = (lax.broadcasted_iota(jnp.int32, (tm, tm), 0)
               > lax.broadcasted_iota(jnp.int32, (tm, tm), 1)).astype(BF16)
    before = jnp.dot(earlier, member.astype(BF16), preferred_element_type=F32) + cnt_ref[0:1, :]
    rank_out = jnp.zeros(logits.shape, jnp.int32)
    for k in range(TOP_K):
        rk = jnp.sum(jnp.where(sel[k], before, 0.0), axis=-1, keepdims=True)
        rank_out = jnp.where(lane == k, rk.astype(jnp.int32), rank_out)
    rank_ref[...] = rank_out
    cnt_ref[...] = cnt_ref[...] + jnp.sum(member, axis=0, keepdims=True)


def _outproj_both_kernel(odp_ref, ogp_ref, xp_ref, ods_ref, ogs_ref, xs_ref, w1_ref, w2_ref, fnw_ref,
                         rw_ref, rb_ref, *out_refs, npb):
    i = pl.program_id(0)

    @pl.when(i < npb)
    def _():
        _outproj_kernel(odp_ref, ogp_ref, xp_ref, w1_ref, w2_ref, fnw_ref, rw_ref, rb_ref, *out_refs)

    @pl.when(i >= npb)
    def _():
        _outproj_kernel(ods_ref, ogs_ref, xs_ref, w1_ref, w2_ref, fnw_ref, rw_ref, rb_ref, *out_refs)


def _outproj(od_p, og_p, x_p, od_s, og_s, x_s, w1, w2, fnw, rw_pad, rb_pad):
    n_p, n_s = x_p.shape[0], x_s.shape[0]
    m = n_p + n_s
    tm = min(512, n_s)
    npb = n_p // tm
    prow = lambda i: (jnp.minimum(i, npb - 1), 0)
    srow = lambda i: (jnp.maximum(i - npb, 0), 0)
    row = lambda i: (i, 0)
    const = lambda i: (0, 0)
    return pl.pallas_call(
        functools.partial(_outproj_both_kernel, npb=npb),
        out_shape=(jax.ShapeDtypeStruct((m, D_MODEL), F32),
                   jax.ShapeDtypeStruct((m, D_MODEL // 2), jnp.uint32),
                   jax.ShapeDtypeStruct((m, LANES), jnp.int32),
                   jax.ShapeDtypeStruct((m, LANES), F32),
                   jax.ShapeDtypeStruct((m, LANES), jnp.int32),
                   jax.ShapeDtypeStruct((SUBLANES, LANES), F32)),
        grid=(m // tm,),
        in_specs=[pl.BlockSpec((tm, GROUP_W), prow),
                  pl.BlockSpec((tm, GROUP_W), prow),
                  pl.BlockSpec((tm, D_MODEL), prow),
                  pl.BlockSpec((tm, GROUP_W), srow),
                  pl.BlockSpec((tm, GROUP_W), srow),
                  pl.BlockSpec((tm, D_MODEL), srow),
                  pl.BlockSpec((GROUP_W, D_MODEL), const),
                  pl.BlockSpec((GROUP_W, D_MODEL), const),
                  pl.BlockSpec((1, D_MODEL), const),
                  pl.BlockSpec((D_MODEL, LANES), const),
                  pl.BlockSpec((1, LANES), const)],
        out_specs=(pl.BlockSpec((tm, D_MODEL), row),
                   pl.BlockSpec((tm, D_MODEL // 2), row),
                   pl.BlockSpec((tm, LANES), row),
                   pl.BlockSpec((tm, LANES), row),
                   pl.BlockSpec((tm, LANES), row),
                   pl.BlockSpec((SUBLANES, LANES), const)),
        compiler_params=_cparams(("arbitrary",)),
        name="outproj_router",
    )(od_p, og_p, x_p, od_s, og_s, x_s, w1, w2, fnw, rw_pad, rb_pad)


def _cast_kernel(w_ref, o_ref):
    o_ref[...] = w_ref[...].astype(BF16)


def _cast_bf16(w):
    e, r, c = w.shape
    tr = 512
    spec = pl.BlockSpec((1, tr, c), lambda i, j: (i, j, 0))
    return pl.pallas_call(
        _cast_kernel,
        out_shape=jax.ShapeDtypeStruct(w.shape, BF16),
        grid=(e, r // tr),
        in_specs=[spec],
        out_specs=spec,
        compiler_params=_cparams(("arbitrary", "arbitrary")),
        name="cast_bf16",
    )(w)


def _ffn_kernel(be_ref, nu_ref, tok_hbm, h2_hbm, wg_ref, bg_ref, wu_ref, bu_ref, wd_ref, bd_ref, y_ref,
                xa_ref, xb_ref, xbf_ref, ia_ref, ib_ref, isem, rsem, *, nf):
    blk = pl.program_id(0)
    f = pl.program_id(1)
    nu = nu_ref[0]
    slot = blk % 2
    xbufs = (xa_ref, xb_ref)
    idxs = (ia_ref, ib_ref)

    def idx_copy(b, s):
        return pltpu.make_async_copy(tok_hbm.at[b], idxs[s], isem.at[s])

    def issue_rows(s):
        def body(g, carry):
            r0 = pl.multiple_of(g * SUBLANES, SUBLANES)
            for u in range(SUBLANES):
                tok = idxs[s][r0 + u]
                pltpu.make_async_copy(h2_hbm.at[pl.ds(tok, 1)], xbufs[s].at[pl.ds(r0 + u, 1)],
                                      rsem.at[s]).start()
            return carry

        lax.fori_loop(0, EXPERT_BLOCK // SUBLANES, body, 0)

    def wait_rows(s):
        pltpu.make_async_copy(h2_hbm.at[pl.ds(0, EXPERT_BLOCK)], xbufs[s], rsem.at[s]).wait()

    @pl.when(jnp.logical_and(f == 0, blk == 0))
    def _():
        first = idx_copy(0, 0)
        first.start()
        first.wait()
        issue_rows(0)

        @pl.when(1 < nu)
        def _():
            idx_copy(1, 1).start()

    for s in (0, 1):
        @pl.when(jnp.logical_and(jnp.logical_and(f == 0, blk < nu), slot == s))
        def _():
            @pl.when(blk + 1 < nu)
            def _():
                idx_copy(blk + 1, 1 - s).wait()

            wait_rows(s)
            xp = xbufs[s][...]
            half = D_MODEL // 2
            xbf_ref[:, 0:half] = pltpu.bitcast(xp << 16, F32).astype(BF16)
            xbf_ref[:, half:] = pltpu.bitcast(xp & jnp.uint32(0xFFFF0000), F32).astype(BF16)

        @pl.when(jnp.logical_and(jnp.logical_and(f == nf - 1, blk + 2 < nu), slot == s))
        def _():
            idx_copy(blk + 2, s).start()

    share = EXPERT_BLOCK // nf

    def compute(next_slot):
        if next_slot is not None:
            r0 = pl.multiple_of(f * share, share)
            for u in range(share):
                tok = idxs[next_slot][r0 + u]
                pltpu.make_async_copy(h2_hbm.at[pl.ds(tok, 1)], xbufs[next_slot].at[pl.ds(r0 + u, 1)],
                                      rsem.at[next_slot]).start()
        x = xbf_ref[...]
        gate = jnp.minimum(jnp.dot(x, wg_ref[0], preferred_element_type=F32) + bg_ref[0],
                           SWIGLU_LIMIT)
        up = jnp.clip(jnp.dot(x, wu_ref[0], preferred_element_type=F32) + bu_ref[0],
                      -SWIGLU_LIMIT, SWIGLU_LIMIT)
        act = (up + 1.0) * gate * jax.nn.sigmoid(SWIGLU_ALPHA * gate)
        part = jnp.dot(act.astype(BF16), wd_ref[0], preferred_element_type=F32)

        @pl.when(f == 0)
        def _():
            y_ref[...] = part + bd_ref[0]

        @pl.when(f > 0)
        def _():
            y_ref[...] += part

    for s in (0, 1):
        @pl.when(jnp.logical_and(blk + 1 < nu, slot == s))
        def _():
            compute(1 - s)

    @pl.when(jnp.logical_and(blk < nu, blk + 1 >= nu))
    def _():
        compute(None)

    @pl.when(jnp.logical_and(blk >= nu, f == 0))
    def _():
        y_ref[...] = jnp.zeros(y_ref.shape, F32)


def _moe_ffn(block_e, n_used, slot_tok, h2, wg, bg, wu, bu, wd, bd):
    n_slots = slot_tok.shape[0]
    nb = n_slots // EXPERT_BLOCK
    nf = D_FF // FFN_TF

    def f_of(b, f, nu):
        return jnp.where(b < nu[0], f, nf - 1)

    grid_spec = pltpu.PrefetchScalarGridSpec(
        num_scalar_prefetch=2,
        grid=(nb, nf),
        in_specs=[pl.BlockSpec(memory_space=pl.ANY),
                  pl.BlockSpec(memory_space=pl.ANY),
                  pl.BlockSpec((1, D_MODEL, FFN_TF), lambda b, f, be, nu: (be[b], 0, f_of(b, f, nu))),
                  pl.BlockSpec((1, 1, FFN_TF), lambda b, f, be, nu: (be[b], 0, f_of(b, f, nu))),
                  pl.BlockSpec((1, D_MODEL, FFN_TF), lambda b, f, be, nu: (be[b], 0, f_of(b, f, nu))),
                  pl.BlockSpec((1, 1, FFN_TF), lambda b, f, be, nu: (be[b], 0, f_of(b, f, nu))),
                  pl.BlockSpec((1, FFN_TF, D_MODEL), lambda b, f, be, nu: (be[b], f_of(b, f, nu), 0)),
                  pl.BlockSpec((1, 1, D_MODEL), lambda b, f, be, nu: (be[b], 0, 0))],
        out_specs=pl.BlockSpec((EXPERT_BLOCK, D_MODEL), lambda b, f, be, nu: (b, 0)),
        scratch_shapes=[pltpu.VMEM((EXPERT_BLOCK, D_MODEL // 2), jnp.uint32),
                        pltpu.VMEM((EXPERT_BLOCK, D_MODEL // 2), jnp.uint32),
                        pltpu.VMEM((EXPERT_BLOCK, D_MODEL), BF16),
                        pltpu.SMEM((EXPERT_BLOCK,), jnp.int32),
                        pltpu.SMEM((EXPERT_BLOCK,), jnp.int32),
                        pltpu.SemaphoreType.DMA((2,)),
                        pltpu.SemaphoreType.DMA((2,))],
    )
    return pl.pallas_call(
        functools.partial(_ffn_kernel, nf=nf),
        out_shape=jax.ShapeDtypeStruct((n_slots, D_MODEL), F32),
        grid_spec=grid_spec,
        compiler_params=_cparams(("arbitrary", "arbitrary")),
        name="moe_ffn",
    )(block_e, n_used, slot_tok.reshape(nb, EXPERT_BLOCK), h2, wg, bg, wu, bu, wd, bd)


def _combine_kernel(pos_hbm, ys_hbm, x1_ref, gate_ref, fnw_ref, y_ref, pa_ref, pb_ref, ya_ref, yb_ref,
                    isem, rsem, *, tm, nb):
    blk = pl.program_id(0)
    slot = blk % 2
    poss = (pa_ref, pb_ref)
    ybufs = (ya_ref, yb_ref)

    def idx_copy(b, s):
        return pltpu.make_async_copy(pos_hbm.at[b], poss[s], isem.at[s])

    def issue_rows(s):
        def body(g, carry):
            t0 = pl.multiple_of(g * SUBLANES, SUBLANES)
            for u in range(SUBLANES):
                for k in range(TOP_K):
                    p = poss[s][(t0 + u) * TOP_K + k]
                    pltpu.make_async_copy(ys_hbm.at[pl.ds(p, 1)], ybufs[s].at[k, pl.ds(t0 + u, 1)],
                                          rsem.at[s]).start()
            return carry

        lax.fori_loop(0, tm // SUBLANES, body, 0)

    def wait_rows(s):
        for k in range(TOP_K):
            pltpu.make_async_copy(ys_hbm.at[pl.ds(0, tm)], ybufs[s].at[k], rsem.at[s]).wait()

    @pl.when(blk == 0)
    def _():
        first = idx_copy(0, 0)
        first.start()
        first.wait()
        issue_rows(0)
        if nb > 1:
            idx_copy(1, 1).start()

    for s in (0, 1):
        @pl.when(jnp.logical_and(blk + 1 < nb, slot == s))
        def _():
            idx_copy(blk + 1, 1 - s).wait()
            issue_rows(1 - s)

        @pl.when(jnp.logical_and(blk + 2 < nb, slot == s))
        def _():
            idx_copy(blk + 2, s).start()

        @pl.when(slot == s)
        def _():
            wait_rows(s)
            g = gate_ref[...]
            moe = g[:, 0:1] * ybufs[s][0]
            for k in range(1, TOP_K):
                moe = moe + g[:, k:k + 1] * ybufs[s][k]
            x2 = x1_ref[...] + moe
            y_ref[...] = (x2 * lax.rsqrt(jnp.mean(x2 * x2, axis=-1, keepdims=True) + EPS)
                          * fnw_ref[...])


def _combine(pos, y_slots, x1, gates, final_norm_w, row0, m):
    tm = min(COMBINE_TM, m)
    nb = m // tm
    off = row0 // tm
    row = lambda i: (i + off, 0)
    return pl.pallas_call(
        functools.partial(_combine_kernel, tm=tm, nb=nb),
        out_shape=jax.ShapeDtypeStruct((m, D_MODEL), F32),
        grid=(nb,),
        in_specs=[pl.BlockSpec(memory_space=pl.ANY),
                  pl.BlockSpec(memory_space=pl.ANY),
                  pl.BlockSpec((tm, D_MODEL), row),
                  pl.BlockSpec((tm, LANES), row),
                  pl.BlockSpec((1, D_MODEL), lambda i: (0, 0))],
        out_specs=pl.BlockSpec((tm, D_MODEL), lambda i: (i, 0)),
        scratch_shapes=[pltpu.SMEM((tm * TOP_K,), jnp.int32),
                        pltpu.SMEM((tm * TOP_K,), jnp.int32),
                        pltpu.VMEM((TOP_K, tm, D_MODEL), F32),
                        pltpu.VMEM((TOP_K, tm, D_MODEL), F32),
                        pltpu.SemaphoreType.DMA((2,)),
                        pltpu.SemaphoreType.DMA((2,))],
        compiler_params=_cparams(("arbitrary",)),
        name="moe_combine",
    )(pos.reshape(nb, tm * TOP_K), y_slots, x1, gates, final_norm_w.reshape(1, D_MODEL))


def _route(streams):
    experts = jnp.arange(N_EXPERTS, dtype=jnp.int32)
    counts = sum(c for _, _, c in streams)
    padded = (counts + EXPERT_BLOCK - 1) // EXPERT_BLOCK * EXPERT_BLOCK
    padded_end = jnp.cumsum(padded)
    base = padded_end - padded
    dests = []
    for top_idx, rank, c in streams:
        onehot = (top_idx[:, :, None] == experts[None, None, :]).astype(jnp.int32)
        dests.append((jnp.sum(onehot * base[None, None, :], axis=-1) + rank).reshape(-1).astype(jnp.int32))
        base = base + c
    n_assign = sum(d.shape[0] for d in dests)
    n_blocks = -(-n_assign // EXPERT_BLOCK) + N_EXPERTS
    flat_tok = jnp.arange(n_assign, dtype=jnp.int32) // TOP_K
    slot_tok = jnp.zeros((n_blocks * EXPERT_BLOCK,), jnp.int32).at[jnp.concatenate(dests)].set(
        flat_tok, mode="promise_in_bounds", unique_indices=True)
    starts = jnp.arange(n_blocks, dtype=jnp.int32) * EXPERT_BLOCK
    block_e = jnp.minimum(jnp.sum((padded_end[None, :] <= starts[:, None]).astype(jnp.int32), axis=1),
                          N_EXPERTS - 1).astype(jnp.int32)
    n_used = (padded_end[-1] // EXPERT_BLOCK).astype(jnp.int32).reshape(1)
    return slot_tok, dests, block_e, n_used


def _mixer(x, pos, k_cache, v_cache, conv_past, s0, lam, attn_norm_w, w_perm, conv_w_t, neg_a, dt_b,
           diff_norm_w, gdn_norm_w):
    b, t, _ = x.shape
    m = b * t
    tm = min(1024, m)
    tabs = _rope_tables(pos)
    if t < tm:
        tabs = tuple(jnp.tile(a, (tm // t, 1)) for a in tabs)
    k_out, v_out, rest, qkv = _inproj(x.reshape(m, D_MODEL), attn_norm_w, w_perm, tabs)
    proj3 = rest.reshape(b, t, REST_COLS)
    qkv3 = qkv.reshape(b, t, 3 * GROUP_W)
    k_rows = k_out.reshape(b, t, N_HEADS, HEAD)
    v_rows = v_out.reshape(b, t, N_HEADS, HEAD)

    if k_cache is None:
        tk = min(512, t // 4)
        vt = qkv3[:, :, 2 * GROUP_W:].reshape(b, t // tk, tk, N_HEADS, HEAD).transpose(0, 3, 1, 4, 2)
        o_d = _diff_attn_prompt(lam, qkv3, vt, diff_norm_w, tk=tk)
    else:
        past = k_cache.shape[1]
        k_all = jnp.concatenate([k_cache.reshape(b, past, GROUP_W).astype(BF16),
                                 qkv3[:, :, GROUP_W:2 * GROUP_W]], axis=1)
        v_all = jnp.concatenate([v_cache.reshape(b, past, N_HEADS, HEAD).astype(BF16),
                                 qkv3[:, :, 2 * GROUP_W:].reshape(b, t, N_HEADS, HEAD)], axis=1)
        vt = v_all.transpose(0, 2, 3, 1)[:, :, None]
        o_d = _diff_attn_full(lam, qkv3, k_all, vt, diff_norm_w)

    conv_past8 = jnp.pad(conv_past, ((0, 0), (SUBLANES - (CONV_K - 1), 0), (0, 0)))
    o_g, s_new, conv8 = _gdn(proj3, conv_past8, s0, conv_w_t, neg_a, dt_b, gdn_norm_w)
    conv_new = conv8[:, SUBLANES - (CONV_K - 1):, :]
    return o_d.reshape(m, GROUP_W), o_g.reshape(m, GROUP_W), k_rows, v_rows, conv_new, s_new


def kernel(x_prompt, x_sample, cache_k_diff, cache_v_diff, state_conv, state_gdn, attn_norm_w, w_in,
           conv_w, a_log, dt_bias, lambda_q1, lambda_k1, lambda_q2, lambda_k2, diff_norm_w, gdn_norm_w,
           w_out, ffn_norm_w, router_w, router_b, w_gate, b_gate, w_up, b_up, w_down, b_down,
           final_norm_w):
    l = 0
    bp, tp, _ = x_prompt.shape
    bs, ts, _ = x_sample.shape
    past_len = cache_k_diff.shape[2]

    lam = (jnp.exp(jnp.sum(lambda_q1[l] * lambda_k1[l])) - jnp.exp(jnp.sum(lambda_q2[l] * lambda_k2[l]))
           + LAM_INIT).astype(F32).reshape(1, 1)

    wl = w_in[l]
    ncol = wl.shape[1]
    w_perm = jnp.concatenate(
        [wl[:, :COL_Z], wl[:, ncol - GROUP_W:], wl[:, COL_Z:COL_Z + 2 * N_HEADS],
         jnp.zeros((D_MODEL, PROJ_COLS - ncol), F32)], axis=1).astype(BF16)
    conv_w_t = conv_w[l].T
    pad8 = lambda a: jnp.pad(a.astype(F32), (0, LANES - N_HEADS)).reshape(1, LANES)
    neg_a = pad8(-jnp.exp(a_log[l]))
    dt_b = pad8(dt_bias[l])
    w1 = w_out[l][:GROUP_W].astype(BF16)
    w2 = w_out[l][GROUP_W:].astype(BF16)
    rw_pad = jnp.pad(router_w[l], ((0, 0), (0, LANES - N_EXPERTS)))
    rb_pad = jnp.concatenate([router_b[l].astype(F32),
                              jnp.full((LANES - N_EXPERTS,), -jnp.inf, F32)]).reshape(1, LANES)
    wg = _cast_bf16(w_gate[l])
    wu = _cast_bf16(w_up[l])
    wd = _cast_bf16(w_down[l])
    bg = b_gate[l].reshape(N_EXPERTS, 1, D_FF)
    bu = b_up[l].reshape(N_EXPERTS, 1, D_FF)
    bd = b_down[l].reshape(N_EXPERTS, 1, D_MODEL)

    mix_args = (lam, attn_norm_w[l], w_perm, conv_w_t, neg_a, dt_b, diff_norm_w[l], gdn_norm_w[l])
    conv0 = jnp.zeros((bp, CONV_K - 1, 3 * GROUP_W), F32)
    s0 = jnp.zeros((bp, N_HEADS, HEAD, HEAD), F32)
    od_p, og_p, kp, vp, cp, sp = _mixer(x_prompt, jnp.arange(tp), None, None, conv0, s0, *mix_args)
    od_s, og_s, ks, vs, cs, ss = _mixer(x_sample, past_len + jnp.arange(ts), cache_k_diff[l],
                                        cache_v_diff[l], state_conv[l], state_gdn[l], *mix_args)

    fnw = ffn_norm_w[l].reshape(1, D_MODEL)
    n_p, n_s = bp * tp, bs * ts
    x1, h2, idx, gate, rank, cnt = _outproj(
        od_p, og_p, x_prompt.reshape(n_p, D_MODEL), od_s, og_s, x_sample.reshape(n_s, D_MODEL),
        w1, w2, fnw, rw_pad, rb_pad)
    slot_tok, (dest,), block_e, n_used = _route(
        [(idx[:, :TOP_K], rank[:, :TOP_K], cnt[0, :N_EXPERTS].astype(jnp.int32))])
    y_slots = _moe_ffn(block_e, n_used, slot_tok, h2, wg, bg, wu, bu, wd, bd)
    y_p = _combine(dest[:n_p * TOP_K], y_slots, x1, gate, final_norm_w, 0, n_p)
    y_s = _combine(dest[n_p * TOP_K:], y_slots, x1, gate, final_norm_w, n_p, n_s)

    return (y_p.reshape(bp, tp, D_MODEL), y_s.reshape(bs, ts, D_MODEL),
            kp[None], vp[None], cp[None], sp[None], ks[None], vs[None], cs[None], ss[None])
```
